```python
import math
import jax, jax.numpy as jnp
from jax import lax
import numpy as np

D_MODEL = 1024
BATCH = 32
SEQ = 2048
DEPTH = 2

EPS = 1e-6
Q_BLOCK = 128
ROPE_THETA = 500000.0
ROPE_FRACTION = 4

A_HEADS = 4
A_QK_DIM = 64
A_V_DIM = 2 * A_QK_DIM
A_QK_W = A_HEADS * 2 * A_QK_DIM
A_V_W = A_HEADS * A_V_DIM

B_HEADS = 4
B_K_DIM = 128
B_V_DIM = 128
B_K_W = B_HEADS * B_K_DIM
B_V_W = B_HEADS * B_V_DIM
B_CONV_CH = 2 * B_K_W + B_V_W
CONV_K = 4
CHUNK = 64

MIX_AB = A_V_W + B_V_W
AB_SIZES = (A_QK_W, A_QK_W, A_V_W, A_V_W, B_CONV_CH, B_V_W, B_HEADS, B_HEADS)
AB_SPLITS = tuple(int(s) for s in np.cumsum(AB_SIZES)[:-1])
IN_AB = int(sum(AB_SIZES))

C_HEADS = 16
C_HEAD_DIM = 64
MIX_C = C_HEADS * C_HEAD_DIM
C_SIZES = (MIX_C, MIX_C, MIX_C, MIX_C, C_HEADS)
C_SPLITS = tuple(int(s) for s in np.cumsum(C_SIZES)[:-1])
IN_C = int(sum(C_SIZES))

kernel_name = "hybrid_diffattn_gdn_fox_block"


def rmsnorm(x, w):
    xf = x.astype(jnp.float32)
    y = xf * lax.rsqrt(jnp.mean(xf * xf, axis=-1, keepdims=True) + EPS)
    return (y * w.astype(jnp.float32)).astype(x.dtype)


def partial_rope(x, positions):
    rot = x.shape[-1] // ROPE_FRACTION
    half = rot // 2
    inv_freq = ROPE_THETA ** (-(jnp.arange(half, dtype=jnp.float32) * 2.0) / rot)
    ang = positions.astype(jnp.float32)[..., None] * inv_freq
    cos = jnp.cos(ang)[:, :, None, :]
    sin = jnp.sin(ang)[:, :, None, :]
    xf = x.astype(jnp.float32)
    x1, x2 = xf[..., :half], xf[..., half:rot]
    out = jnp.concatenate([x1 * cos - x2 * sin, x2 * cos + x1 * sin, xf[..., rot:]], axis=-1)
    return out.astype(x.dtype)


def causal_block_mask(s0, s1):
    qpos = jnp.arange(s0, s1)
    kpos = jnp.arange(s1)
    return kpos[None, :] <= qpos[:, None]


def diff_attention(q, k, v, lam):
    seq = q.shape[1]
    scale = q.shape[-1] ** -0.5
    outs = []
    for s0 in range(0, seq, Q_BLOCK):
        s1 = s0 + Q_BLOCK
        logits = jnp.einsum('bqhcd,bkhcd->bhcqk', q[:, s0:s1], k[:, :s1],
                            preferred_element_type=jnp.float32) * scale
        logits = jnp.where(causal_block_mask(s0, s1), logits, -jnp.inf)
        p = jax.nn.softmax(logits, axis=-1)
        p = p[:, :, 0] - lam * p[:, :, 1]
        outs.append(jnp.einsum('bhqk,bkhe->bqhe', p, v[:, :s1].astype(jnp.float32)))
    return jnp.concatenate(outs, axis=1).astype(v.dtype)


def forgetting_attention(q, k, v, cum_logf):
    seq = q.shape[1]
    scale = q.shape[-1] ** -0.5
    c = jnp.transpose(cum_logf, (0, 2, 1))
    outs = []
    for s0 in range(0, seq, Q_BLOCK):
        s1 = s0 + Q_BLOCK
        logits = jnp.einsum('bqhd,bkhd->bhqk', q[:, s0:s1], k[:, :s1],
                            preferred_element_type=jnp.float32) * scale
        bias = c[:, :, s0:s1, None] - c[:, :, None, :s1]
        logits = jnp.where(causal_block_mask(s0, s1), logits + bias, -jnp.inf)
        p = jax.nn.softmax(logits, axis=-1)
        outs.append(jnp.einsum('bhqk,bkhd->bqhd', p, v[:, :s1].astype(jnp.float32)))
    return jnp.concatenate(outs, axis=1).astype(v.dtype)


def causal_depthwise_conv(x, w):
    kw = w.shape[0]
    return lax.conv_general_dilated(
        x, w[:, None, :], window_strides=(1,), padding=[(kw - 1, 0)],
        dimension_numbers=('NWC', 'WIO', 'NWC'), feature_group_count=x.shape[-1])


def gated_delta_rule_chunked(q, k, v, g, beta):
    out_dtype = v.dtype
    bsz, seq, heads, dk = q.shape
    dv = v.shape[-1]
    n_chunks = seq // CHUNK

    def to_chunks(t):
        return t.astype(jnp.float32).reshape(bsz, n_chunks, CHUNK, heads, -1).transpose(1, 0, 3, 2, 4)

    def to_chunks_s(t):
        return t.reshape(bsz, n_chunks, CHUNK, heads).transpose(1, 0, 3, 2)

    qc = to_chunks(q) * (dk ** -0.5)
    kc = to_chunks(k)
    vc = to_chunks(v)
    bc = to_chunks_s(beta)
    gc = jnp.cumsum(to_chunks_s(g), axis=-1)

    tril = jnp.tril(jnp.ones((CHUNK, CHUNK), dtype=bool))
    strict = jnp.tril(jnp.ones((CHUNK, CHUNK), dtype=bool), k=-1)
    decay = jnp.exp(jnp.where(tril, gc[..., :, None] - gc[..., None, :], -jnp.inf))

    k_beta = kc * bc[..., None]
    v_beta = vc * bc[..., None]
    lower = jnp.where(strict, jnp.einsum('nbhck,nbhsk->nbhcs', k_beta, kc) * decay, 0.0)
    unit_lower = lower + jnp.eye(CHUNK, dtype=jnp.float32)
    rhs = jnp.concatenate([v_beta, k_beta * jnp.exp(gc)[..., None]], axis=-1)
    sol = lax.linalg.triangular_solve(unit_lower, rhs, left_side=True, lower=True,
                                      unit_diagonal=True)
    u, w = sol[..., :dv], sol[..., dv:]
    qk = jnp.where(tril, jnp.einsum('nbhck,nbhsk->nbhcs', qc, kc) * decay, 0.0)

    def step(state, xs):
        q_c, k_c, u_c, w_c, g_c, qk_c = xs
        v_new = u_c - jnp.einsum('bhck,bhkv->bhcv', w_c, state)
        o = (jnp.einsum('bhck,bhkv->bhcv', q_c * jnp.exp(g_c)[..., None], state)
             + jnp.einsum('bhcs,bhsv->bhcv', qk_c, v_new))
        g_last = g_c[..., -1:]
        state = (state * jnp.exp(g_last)[..., None]
                 + jnp.einsum('bhck,bhcv->bhkv', k_c * jnp.exp(g_last - g_c)[..., None], v_new))
        return state, o

    state0 = jnp.zeros((bsz, heads, dk, dv), jnp.float32)
    _, o = lax.scan(step, state0, (qc, kc, u, w, gc, qk))
    return o.transpose(1, 0, 3, 2, 4).reshape(bsz, seq, heads, dv).astype(out_dtype)


def mixer_ab(h, positions, layer, w_in_ab, a_lambda_q1, a_lambda_k1, a_lambda_q2, a_lambda_k2,
             a_subln, b_conv_w, b_a_log, b_dt_bias, b_head_norm, w_out_ab):
    bsz, seq, _ = h.shape
    proj = h @ w_in_ab
    a_q, a_k, a_v, a_z, b_qkv, b_z, b_beta, b_a = jnp.split(proj, AB_SPLITS, axis=-1)

    lambda_init = 0.8 - 0.6 * math.exp(-0.3 * layer)
    f32 = jnp.float32
    lam = (jnp.exp(jnp.sum(a_lambda_q1.astype(f32) * a_lambda_k1.astype(f32)))
           - jnp.exp(jnp.sum(a_lambda_q2.astype(f32) * a_lambda_k2.astype(f32))) + lambda_init)
    q = partial_rope(a_q.reshape(bsz, seq, A_HEADS * 2, A_QK_DIM), positions)
    k = partial_rope(a_k.reshape(bsz, seq, A_HEADS * 2, A_QK_DIM), positions)
    q = q.reshape(bsz, seq, A_HEADS, 2, A_QK_DIM)
    k = k.reshape(bsz, seq, A_HEADS, 2, A_QK_DIM)
    v = a_v.reshape(bsz, seq, A_HEADS, A_V_DIM)
    o_a = diff_attention(q, k, v, lam)
    o_a = rmsnorm(o_a, a_subln) * (1.0 - lambda_init)
    o_a = o_a.reshape(bsz, seq, A_V_W) * jax.nn.silu(a_z)

    qkv = jax.nn.silu(causal_depthwise_conv(b_qkv, b_conv_w))
    bq, bk, bv = jnp.split(qkv, (B_K_W, 2 * B_K_W), axis=-1)
    bq = bq.reshape(bsz, seq, B_HEADS, B_K_DIM).astype(f32)
    bk = bk.reshape(bsz, seq, B_HEADS, B_K_DIM).astype(f32)
    bv = bv.reshape(bsz, seq, B_HEADS, B_V_DIM)
    bq = bq * lax.rsqrt(jnp.sum(bq * bq, axis=-1, keepdims=True) + EPS)
    bk = bk * lax.rsqrt(jnp.sum(bk * bk, axis=-1, keepdims=True) + EPS)
    beta = jax.nn.sigmoid(b_beta.astype(f32))
    g = -jnp.exp(b_a_log.astype(f32)) * jax.nn.softplus(b_a.astype(f32) + b_dt_bias.astype(f32))
    o_b = gated_delta_rule_chunked(bq, bk, bv, g, beta)
    o_b = rmsnorm(o_b, b_head_norm).reshape(bsz, seq, B_V_W) * jax.nn.silu(b_z)

    return jnp.concatenate([o_a, o_b], axis=-1) @ w_out_ab


def mixer_c(h, w_in_c, c_forget_bias, w_out_c):
    bsz, seq, _ = h.shape
    proj = h @ w_in_c
    q, k, v, z, f_logit = jnp.split(proj, C_SPLITS, axis=-1)
    log_f = jax.nn.log_sigmoid(f_logit.astype(jnp.float32) + c_forget_bias.astype(jnp.float32))
    cum_logf = jnp.cumsum(log_f, axis=1)
    o = forgetting_attention(q.reshape(bsz, seq, C_HEADS, C_HEAD_DIM),
                             k.reshape(bsz, seq, C_HEADS, C_HEAD_DIM),
                             v.reshape(bsz, seq, C_HEADS, C_HEAD_DIM), cum_logf)
    return (o.reshape(bsz, seq, MIX_C) * jax.nn.silu(z)) @ w_out_c


def setup_inputs(seed: int = 0) -> dict:
    key = jax.random.key(seed)
    ks = jax.random.split(key, 20)
    f32 = jnp.float32

    def nrm(k, shape, scale):
        return jax.random.normal(k, shape, f32) * scale

    x = nrm(ks[0], (BATCH, SEQ, D_MODEL), 1.0)
    positions = jnp.tile(jnp.arange(SEQ, dtype=jnp.int32)[None, :], (BATCH, 1))
    pre_norm = 1.0 + nrm(ks[1], (DEPTH, D_MODEL), 0.05)
    post_norm = 1.0 + nrm(ks[2], (DEPTH, D_MODEL), 0.05)
    w_in_ab = nrm(ks[3], (D_MODEL, IN_AB), D_MODEL ** -0.5)
    a_lambda_q1 = nrm(ks[4], (A_QK_DIM,), 0.1)
    a_lambda_k1 = nrm(ks[5], (A_QK_DIM,), 0.1)
    a_lambda_q2 = nrm(ks[6], (A_QK_DIM,), 0.1)
    a_lambda_k2 = nrm(ks[7], (A_QK_DIM,), 0.1)
    a_subln = 1.0 + nrm(ks[8], (A_V_DIM,), 0.05)
    b_conv_w = nrm(ks[9], (CONV_K, B_CONV_CH), CONV_K ** -0.5)
    b_a_log = jnp.log(jax.random.uniform(ks[10], (B_HEADS,), f32, 1.0, 16.0))
    dt = jnp.exp(jax.random.uniform(ks[11], (B_HEADS,), f32, math.log(1e-3), math.log(1e-1)))
    b_dt_bias = dt + jnp.log(-jnp.expm1(-dt))
    b_head_norm = 1.0 + nrm(ks[12], (B_V_DIM,), 0.05)
    w_out_ab = nrm(ks[13], (MIX_AB, D_MODEL), MIX_AB ** -0.5)
    w_in_c = nrm(ks[14], (D_MODEL, IN_C), D_MODEL ** -0.5)
    c_forget_bias = jax.random.uniform(ks[15], (C_HEADS,), f32, 1.0, 5.0)
    w_out_c = nrm(ks[16], (MIX_C, D_MODEL), MIX_C ** -0.5)
    return {"x": x, "positions": positions, "pre_norm": pre_norm, "post_norm": post_norm,
            "w_in_ab": w_in_ab, "a_lambda_q1": a_lambda_q1, "a_lambda_k1": a_lambda_k1,
            "a_lambda_q2": a_lambda_q2, "a_lambda_k2": a_lambda_k2, "a_subln": a_subln,
            "b_conv_w": b_conv_w, "b_a_log": b_a_log, "b_dt_bias": b_dt_bias,
            "b_head_norm": b_head_norm, "w_out_ab": w_out_ab,
            "w_in_c": w_in_c, "c_forget_bias": c_forget_bias, "w_out_c": w_out_c}


def reference(x, positions, pre_norm, post_norm, w_in_ab, a_lambda_q1, a_lambda_k1, a_lambda_q2,
              a_lambda_k2, a_subln, b_conv_w, b_a_log, b_dt_bias, b_head_norm, w_out_ab,
              w_in_c, c_forget_bias, w_out_c):
    for layer in range(DEPTH):
        h = rmsnorm(x, pre_norm[layer])
        if layer % 2 == 0:
            y = mixer_ab(h, positions, layer, w_in_ab, a_lambda_q1, a_lambda_k1, a_lambda_q2,
                         a_lambda_k2, a_subln, b_conv_w, b_a_log, b_dt_bias, b_head_norm, w_out_ab)
        else:
            y = mixer_c(h, w_in_c, c_forget_bias, w_out_c)
        x = x + rmsnorm(y, post_norm[layer])
    return x
```

```python
import functools
import math

import jax
import jax.numpy as jnp
from jax import lax
from jax.experimental import pallas as pl
from jax.experimental.pallas import tpu as pltpu

F32 = jnp.float32
BF16 = jnp.bfloat16

EPS = 1e-6
ROPE_THETA = 500000.0
ROT_DIMS = 16
ROT_HALF = ROT_DIMS // 2
SUB_DIM = 64
LANES = 128
N_GROUPS = 4
SEC = N_GROUPS * LANES
CONV_K = 4
CHUNK = 64
GDN_HEADS = 4
FOX_PAIRS = 8
LAMBDA_INIT_0 = 0.8 - 0.6 * math.exp(-0.3 * 0)
VMEM_LIMIT = 48 * 1024 * 1024

TM_PROJ = 256
TM_OUT = 512
TQ = 256


def _rmsnorm_rows(x, w):
    return x * lax.rsqrt(jnp.mean(x * x, axis=-1, keepdims=True) + EPS) * w


def _silu(x):
    return x * jax.nn.sigmoid(x)


def _softplus(x):
    return jnp.maximum(x, 0.0) + jnp.log(1.0 + jnp.exp(-jnp.abs(x)))


def _prefix_sum_rows(x, period):
    rows = x.shape[0]
    pos = lax.broadcasted_iota(jnp.int32, x.shape, 0) % period
    d = 1
    while d < min(period, rows):
        x = x + jnp.where(pos >= d, pltpu.roll(x, d, 0), 0.0)
        d *= 2
    return x


def _proj_ab_kernel(x_ref, pos_ref, invf_ref, nw_ref, w_ref, cw_ref, alog_ref, dtb_ref,
                    q_ref, k_ref, v_ref, az_ref, bq_ref, bk_ref, bv_ref, bz_ref, bg_ref,
                    cbuf_ref, *, tiles_per_seq):
    tm = x_ref.shape[0]
    i = pl.program_id(0)
    h = _rmsnorm_rows(x_ref[...], nw_ref[...]).astype(BF16)

    def proj(lo, width):
        return jnp.dot(h, w_ref[:, lo:lo + width], preferred_element_type=F32)

    ang = pos_ref[...].astype(F32) * invf_ref[...]
    cos = jnp.cos(ang)
    sin = jnp.sin(ang)
    d = lax.broadcasted_iota(jnp.int32, (1, LANES), 1) % SUB_DIM
    sin_hi = jnp.where((d >= ROT_HALF) & (d < ROT_DIMS), sin, 0.0)
    sin_lo = jnp.where(d < ROT_HALF, -sin, 0.0)

    def rope_store(acc, out_ref, scale):
        for g in range(N_GROUPS):
            a = acc[:, g * LANES:(g + 1) * LANES]
            r = (a * cos + pltpu.roll(a, ROT_HALF, 1) * sin_hi
                 + pltpu.roll(a, LANES - ROT_HALF, 1) * sin_lo)
            out_ref[:, g * LANES:(g + 1) * LANES] = (r * scale).astype(BF16)

    rope_store(proj(0, SEC), q_ref, SUB_DIM ** -0.5)
    rope_store(proj(SEC, SEC), k_ref, 1.0)
    v_ref[...] = proj(2 * SEC, SEC).astype(BF16)
    az_ref[...] = proj(3 * SEC, SEC).astype(BF16)

    @pl.when(i % tiles_per_seq == 0)
    def _():
        cbuf_ref[0:8, :] = jnp.zeros((8, 3 * SEC), F32)

    def conv_silu(sec):
        lo = sec * SEC
        cbuf_ref[8:8 + tm, lo:lo + SEC] = proj(4 * SEC + lo, SEC)
        y = jnp.zeros((tm, SEC), F32)
        for j in range(CONV_K):
            y = y + cw_ref[j:j + 1, lo:lo + SEC] * cbuf_ref[pl.ds(8 - (CONV_K - 1) + j, tm), lo:lo + SEC]
        return _silu(y)

    def l2norm_store(y, out_ref, scale):
        for g in range(N_GROUPS):
            a = y[:, g * LANES:(g + 1) * LANES]
            a = a * lax.rsqrt(jnp.sum(a * a, axis=-1, keepdims=True) + EPS)
            out_ref[:, g * LANES:(g + 1) * LANES] = (a * scale).astype(BF16)

    l2norm_store(conv_silu(0), bq_ref, LANES ** -0.5)
    l2norm_store(conv_silu(1), bk_ref, 1.0)
    bv_ref[...] = conv_silu(2).astype(BF16)
    cbuf_ref[0:8, :] = cbuf_ref[tm:tm + 8, :]

    bz_ref[...] = proj(7 * SEC, SEC).astype(BF16)

    sm = proj(8 * SEC, LANES)
    beta = jax.nn.sigmoid(sm)
    g = -jnp.exp(alog_ref[...]) * _softplus(sm + dtb_ref[...])
    lane = lax.broadcasted_iota(jnp.int32, (1, LANES), 1)
    bg_ref[...] = jnp.where(lane < GDN_HEADS, beta, g)


def _proj_ab(x2, pos2, invf, nw, w, cw, alog, dtb, *, seq):
    t, dm = x2.shape
    tm = TM_PROJ
    n_in = w.shape[1]
    row = lambda i: (i, 0)
    fix = lambda i: (0, 0)
    wide = lambda dt: jax.ShapeDtypeStruct((t, SEC), dt)
    out_shape = [wide(BF16)] * 8 + [jax.ShapeDtypeStruct((t, LANES), F32)]
    out_specs = [pl.BlockSpec((tm, SEC), row)] * 8 + [pl.BlockSpec((tm, LANES), row)]
    return pl.pallas_call(
        functools.partial(_proj_ab_kernel, tiles_per_seq=seq // tm),
        grid=(t // tm,),
        in_specs=[
            pl.BlockSpec((tm, dm), row),
            pl.BlockSpec((tm, 1), row),
            pl.BlockSpec((1, LANES), fix),
            pl.BlockSpec((1, dm), fix),
            pl.BlockSpec((dm, n_in), fix),
            pl.BlockSpec((CONV_K, 3 * SEC), fix),
            pl.BlockSpec((1, LANES), fix),
            pl.BlockSpec((1, LANES), fix),
        ],
        out_specs=out_specs,
        out_shape=out_shape,
        scratch_shapes=[pltpu.VMEM((tm + 8, 3 * SEC), F32)],
        compiler_params=pltpu.CompilerParams(
            dimension_semantics=("arbitrary",), vmem_limit_bytes=VMEM_LIMIT),
        name="proj_ab",
    )(x2, pos2, invf, nw, w, cw, alog, dtb)


def _proj_c_kernel(x_ref, nw_ref, w_ref, fb_ref, q_ref, k_ref, v_ref, z_ref, ct_ref, carry_ref,
                   *, tiles_per_seq):
    tm = x_ref.shape[0]
    i = pl.program_id(0)
    width = q_ref.shape[1]
    h = _rmsnorm_rows(x_ref[...], nw_ref[...]).astype(BF16)

    def proj(lo, n):
        return jnp.dot(h, w_ref[:, lo:lo + n], preferred_element_type=F32)

    for sec in range(width // SEC):
        lo = sec * SEC
        q_ref[:, lo:lo + SEC] = (proj(lo, SEC) * SUB_DIM ** -0.5).astype(BF16)
        k_ref[:, lo:lo + SEC] = proj(width + lo, SEC).astype(BF16)
        v_ref[:, lo:lo + SEC] = proj(2 * width + lo, SEC).astype(BF16)
        z_ref[:, lo:lo + SEC] = proj(3 * width + lo, SEC).astype(BF16)

    @pl.when(i % tiles_per_seq == 0)
    def _():
        carry_ref[...] = jnp.zeros_like(carry_ref)

    f = proj(4 * width, LANES) + fb_ref[...]
    log_f = -_softplus(-f)
    c = _prefix_sum_rows(log_f, tm) + carry_ref[0:1, :]
    carry_ref[...] = jnp.broadcast_to(c[tm - 1:tm, :], carry_ref.shape)
    ct_ref[0, 0] = c.T[0:2 * FOX_PAIRS, :]


def _proj_c(x2, nw, w, fb, *, seq):
    t, dm = x2.shape
    tm = TM_PROJ
    width = 2 * FOX_PAIRS * SUB_DIM
    n_in = w.shape[1]
    tiles = seq // tm
    row = lambda i: (i, 0)
    fix = lambda i: (0, 0)
    wide = jax.ShapeDtypeStruct((t, width), BF16)
    return pl.pallas_call(
        functools.partial(_proj_c_kernel, tiles_per_seq=tiles),
        grid=(t // tm,),
        in_specs=[
            pl.BlockSpec((tm, dm), row),
            pl.BlockSpec((1, dm), fix),
            pl.BlockSpec((dm, n_in), fix),
            pl.BlockSpec((1, LANES), fix),
        ],
        out_specs=[pl.BlockSpec((tm, width), row)] * 4
        + [pl.BlockSpec((1, 1, 2 * FOX_PAIRS, tm), lambda i: (i // tiles, i % tiles, 0, 0))],
        out_shape=[wide] * 4
        + [jax.ShapeDtypeStruct((t // seq, tiles, 2 * FOX_PAIRS, tm), F32)],
        scratch_shapes=[pltpu.VMEM((8, LANES), F32)],
        compiler_params=pltpu.CompilerParams(
            dimension_semantics=("arbitrary",), vmem_limit_bytes=VMEM_LIMIT),
        name="proj_c",
    )(x2, nw, w, fb)


def _flash_two_softmax(q_ref, k_ref, v_ref, bias_fn):
    tq = q_ref.shape[1]
    i = pl.program_id(2)
    q = q_ref[0]
    lo_lanes = lax.broadcasted_iota(jnp.int32, (1, LANES), 1) < SUB_DIM
    zero = jnp.zeros_like(q)
    q_sub = (jnp.where(lo_lanes, q, zero), jnp.where(lo_lanes, zero, q))
    causal = (lax.broadcasted_iota(jnp.int32, (tq, tq), 1)
              <= lax.broadcasted_iota(jnp.int32, (tq, tq), 0))

    def tile(j, carry, masked):
        r0 = pl.multiple_of(j * tq, tq)
        k = k_ref[0, pl.ds(r0, tq), :]
        v = v_ref[0, pl.ds(r0, tq), :]
        out = []
        for c in range(2):
            m, l, acc = carry[c]
            s = lax.dot_general(q_sub[c], k, (((1,), (1,)), ((), ())),
                                preferred_element_type=F32)
            if bias_fn is not None:
                s = s + bias_fn(j, c)
            if masked:
                s = jnp.where(causal, s, -jnp.inf)
            m_new = jnp.maximum(m, jnp.max(s, axis=1, keepdims=True))
            alpha = jnp.exp(m - m_new)
            p = jnp.exp(s - m_new)
            l = alpha * l + jnp.sum(p, axis=1, keepdims=True)
            acc = alpha * acc + jnp.dot(p.astype(BF16), v, preferred_element_type=F32)
            out.append((m_new, l, acc))
        return tuple(out)

    init = tuple((jnp.full((tq, 1), -jnp.inf, F32), jnp.zeros((tq, 1), F32),
                  jnp.zeros((tq, LANES), F32)) for _ in range(2))
    carry = lax.fori_loop(0, i, lambda j, c: tile(j, c, False), init)
    carry = tile(i, carry, True)
    return lo_lanes, carry


def _diff_attn_kernel(q_ref, k_ref, v_ref, z_ref, lam_ref, subln_ref, o_ref):
    _, ((_, l0, a0), (_, l1, a1)) = _flash_two_softmax(q_ref, k_ref, v_ref, None)
    lv = lam_ref[...]
    lam = (jnp.exp(jnp.sum(lv[0:1] * lv[1:2], axis=-1, keepdims=True))
           - jnp.exp(jnp.sum(lv[2:3] * lv[3:4], axis=-1, keepdims=True)) + LAMBDA_INIT_0)
    o = a0 / l0 - lam * (a1 / l1)
    o = _rmsnorm_rows(o, subln_ref[...]) * (1.0 - LAMBDA_INIT_0)
    o_ref[0] = (o * _silu(z_ref[0].astype(F32))).astype(BF16)


def _fox_attn_kernel(q_ref, k_ref, v_ref, z_ref, ct_ref, o_ref):
    hp = pl.program_id(1)

    def bias(j, c):
        return -ct_ref[0, j, pl.ds(2 * hp + c, 1), :]

    lo_lanes, ((_, l0, a0), (_, l1, a1)) = _flash_two_softmax(q_ref, k_ref, v_ref, bias)
    o = jnp.where(lo_lanes, a0 / l0, a1 / l1)
    o_ref[0] = (o * _silu(z_ref[0].astype(F32))).astype(BF16)


def _attention(kernel_fn, q, k, v, z, extras, extra_specs, name):
    b, s, width = q.shape
    groups = width // LANES
    tq = TQ
    qspec = pl.BlockSpec((1, tq, LANES), lambda bi, g, i: (bi, i, g))
    kvspec = pl.BlockSpec((1, s, LANES), lambda bi, g, i: (bi, 0, g))
    return pl.pallas_call(
        kernel_fn,
        grid=(b, groups, s // tq),
        in_specs=[qspec, kvspec, kvspec, qspec] + extra_specs,
        out_specs=qspec,
        out_shape=jax.ShapeDtypeStruct((b, s, width), BF16),
        compiler_params=pltpu.CompilerParams(
            dimension_semantics=("arbitrary", "arbitrary", "arbitrary"),
            vmem_limit_bytes=VMEM_LIMIT),
        name=name,
    )(q, k, v, z, *extras)


def _gdn_kernel(q_ref, k_ref, v_ref, z_ref, bg_ref, hn_ref, o_ref, state_ref):
    s = q_ref.shape[1]
    state_ref[...] = jnp.zeros_like(state_ref)
    ri = lax.broadcasted_iota(jnp.int32, (CHUNK, CHUNK), 0)
    ci = lax.broadcasted_iota(jnp.int32, (CHUNK, CHUNK), 1)
    tril = ci <= ri
    strict = ci < ri

    def mm(a, b):
        return jnp.dot(a.astype(BF16), b.astype(BF16), preferred_element_type=F32)

    def chunk_body(c, carry):
        r0 = pl.multiple_of(c * CHUNK, CHUNK)
        rows = pl.ds(r0, CHUNK)
        bg = bg_ref[0, rows, :]
        gcum = _prefix_sum_rows(bg, CHUNK)
        gcum_t = gcum.T
        for h in range(GDN_HEADS):
            lanes = slice(h * LANES, (h + 1) * LANES)
            beta = bg[:, h:h + 1]
            gcol = gcum[:, GDN_HEADS + h:GDN_HEADS + h + 1]
            grow = gcum_t[GDN_HEADS + h:GDN_HEADS + h + 1, :]
            glast = gcol[CHUNK - 1:CHUNK, :]
            q = q_ref[0, rows, lanes]
            k = k_ref[0, rows, lanes]
            kf = k.astype(F32)
            vf = v_ref[0, rows, lanes].astype(F32)
            k_beta = kf * beta
            decay = jnp.exp(jnp.where(tril, gcol - grow, -jnp.inf))
            qk_kk = lax.dot_general(
                jnp.concatenate([q, k_beta.astype(BF16)], axis=0), k,
                (((1,), (1,)), ((), ())), preferred_element_type=F32)
            qk = jnp.where(tril, qk_kk[:CHUNK] * decay, 0.0)
            low = jnp.where(strict, qk_kk[CHUNK:] * decay, 0.0)
            x = jnp.concatenate([vf * beta, k_beta * jnp.exp(gcol)], axis=1)
            x = x - mm(low, x)
            p = low
            for _ in range(5):
                p = mm(p, p)
                x = x + mm(p, x)
            u = x[:, :LANES]
            w = x[:, LANES:]
            st = state_ref[h]
            v_new = u - mm(w, st)
            o = mm(q.astype(F32) * jnp.exp(gcol), st) + mm(qk, v_new)
            k_dec = (kf * jnp.exp(glast - gcol)).astype(BF16)
            state_ref[h] = st * jnp.exp(glast) + lax.dot_general(
                k_dec, v_new.astype(BF16), (((0,), (0,)), ((), ())), preferred_element_type=F32)
            o = _rmsnorm_rows(o, hn_ref[...])
            o_ref[0, rows, lanes] = (o * _silu(z_ref[0, rows, lanes].astype(F32))).astype(BF16)
        return carry

    lax.fori_loop(0, s // CHUNK, chunk_body, 0)


def _gdn(q, k, v, z, bg, hn):
    b, s, width = q.shape
    spec = pl.BlockSpec((1, s, width), lambda bi: (bi, 0, 0))
    return pl.pallas_call(
        _gdn_kernel,
        grid=(b,),
        in_specs=[spec, spec, spec, spec,
                  pl.BlockSpec((1, s, LANES), lambda bi: (bi, 0, 0)),
                  pl.BlockSpec((1, LANES), lambda bi: (0, 0))],
        out_specs=spec,
        out_shape=jax.ShapeDtypeStruct((b, s, width), BF16),
        scratch_shapes=[pltpu.VMEM((GDN_HEADS, LANES, LANES), F32)],
        compiler_params=pltpu.CompilerParams(
            dimension_semantics=("arbitrary",), vmem_limit_bytes=VMEM_LIMIT),
        name="gdn",
    )(q, k, v, z, bg, hn)


def _out_proj_kernel(*refs, n_parts):
    x_ref = refs[0]
    parts = refs[1:1 + n_parts]
    w_ref, pw_ref, o_ref = refs[1 + n_parts:]
    y = None
    lo = 0
    for part in parts:
        n = part.shape[1]
        d = jnp.dot(part[...], w_ref[lo:lo + n, :], preferred_element_type=F32)
        y = d if y is None else y + d
        lo += n
    o_ref[...] = x_ref[...] + _rmsnorm_rows(y, pw_ref[...])


def _out_proj(x2, parts, w, pw, name):
    t, dm = x2.shape
    tm = TM_OUT
    row = lambda i: (i, 0)
    fix = lambda i: (0, 0)
    return pl.pallas_call(
        functools.partial(_out_proj_kernel, n_parts=len(parts)),
        grid=(t // tm,),
        in_specs=[pl.BlockSpec((tm, dm), row)]
        + [pl.BlockSpec((tm, p.shape[1]), row) for p in parts]
        + [pl.BlockSpec(w.shape, fix), pl.BlockSpec((1, dm), fix)],
        out_specs=pl.BlockSpec((tm, dm), row),
        out_shape=jax.ShapeDtypeStruct((t, dm), F32),
        compiler_params=pltpu.CompilerParams(
            dimension_semantics=("arbitrary",), vmem_limit_bytes=VMEM_LIMIT),
        name=name,
    )(x2, *parts, w, pw)


def _pad_lanes(vec, offset=0):
    out = jnp.zeros((1, LANES), F32)
    return out.at[0, offset:offset + vec.shape[0]].set(vec.astype(F32))


def kernel(x, positions, pre_norm, post_norm, w_in_ab, a_lambda_q1, a_lambda_k1, a_lambda_q2,
           a_lambda_k2, a_subln, b_conv_w, b_a_log, b_dt_bias, b_head_norm, w_out_ab,
           w_in_c, c_forget_bias, w_out_c):
    b, s, dm = x.shape
    t = b * s
    x2 = x.reshape(t, dm)

    main = 8 * SEC
    w_ab = jnp.concatenate(
        [w_in_ab[:, :main], jnp.pad(w_in_ab[:, main:], ((0, 0), (0, LANES - 2 * GDN_HEADS)))],
        axis=1).astype(BF16)
    inv_freq = ROPE_THETA ** (-(jnp.arange(ROT_HALF, dtype=F32) * 2.0) / ROT_DIMS)
    d = jnp.arange(LANES) % SUB_DIM
    invf = jnp.where(d < ROT_DIMS, inv_freq[d % ROT_HALF], 0.0).reshape(1, LANES)
    (a_q, a_k, a_v, a_z, b_q, b_k, b_v, b_z, b_bg) = _proj_ab(
        x2, positions.reshape(t, 1), invf, pre_norm[0:1], w_ab, b_conv_w.astype(F32),
        _pad_lanes(b_a_log, GDN_HEADS), _pad_lanes(b_dt_bias, GDN_HEADS), seq=s)

    lam_vecs = jnp.concatenate(
        [_pad_lanes(v) for v in (a_lambda_q1, a_lambda_k1, a_lambda_q2, a_lambda_k2)], axis=0)
    r3 = lambda a: a.reshape(b, s, a.shape[-1])
    o_a = _attention(
        _diff_attn_kernel, r3(a_q), r3(a_k), r3(a_v), r3(a_z),
        [lam_vecs, a_subln.astype(F32).reshape(1, LANES)],
        [pl.BlockSpec((4, LANES), lambda bi, g, i: (0, 0)),
         pl.BlockSpec((1, LANES), lambda bi, g, i: (0, 0))],
        "diff_attn")
    o_b = _gdn(r3(b_q), r3(b_k), r3(b_v), r3(b_z), r3(b_bg),
               b_head_norm.astype(F32).reshape(1, LANES))
    x2 = _out_proj(x2, [o_a.reshape(t, SEC), o_b.reshape(t, SEC)], w_out_ab.astype(BF16),
                   post_norm[0:1], "out_proj_ab")

    width = 2 * FOX_PAIRS * SUB_DIM
    w_c = jnp.concatenate(
        [w_in_c[:, :4 * width],
         jnp.pad(w_in_c[:, 4 * width:], ((0, 0), (0, LANES - 2 * FOX_PAIRS)))], axis=1).astype(BF16)
    c_q, c_k, c_v, c_z, c_t = _proj_c(x2, pre_norm[1:2], w_c, _pad_lanes(c_forget_bias), seq=s)
    tiles = s // TM_PROJ
    o_c = _attention(
        _fox_attn_kernel, r3(c_q), r3(c_k), r3(c_v), r3(c_z), [c_t],
        [pl.BlockSpec((1, tiles, 2 * FOX_PAIRS, TM_PROJ), lambda bi, g, i: (bi, 0, 0, 0))],
        "fox_attn")
    x2 = _out_proj(x2, [o_c.reshape(t, width)], w_out_c.astype(BF16), post_norm[1:2],
                   "out_proj_c")
    return x2.reshape(b, s, dm)
```

```python
import functools
import math

import jax
import jax.numpy as jnp
from jax import lax
from jax.experimental import pallas as pl
from jax.experimental.pallas import tpu as pltpu

F32 = jnp.float32
BF16 = jnp.bfloat16

EPS = 1e-6
ROPE_THETA = 500000.0
ROT_DIMS = 16
ROT_HALF = ROT_DIMS // 2
SUB_DIM = 64
LANES = 128
N_GROUPS = 4
SEC = N_GROUPS * LANES
CONV_K = 4
CHUNK = 64
GDN_HEADS = 4
FOX_PAIRS = 8
LAMBDA_INIT_0 = 0.8 - 0.6 * math.exp(-0.3 * 0)
VMEM_LIMIT = 48 * 1024 * 1024

TM_PROJ = 256
TM_OUT = 512
TQ = 512
LOG2E = math.log2(math.e)
Q_SCALE = SUB_DIM ** -0.5 * LOG2E


def _rmsnorm_rows(x, w):
    return x * lax.rsqrt(jnp.mean(x * x, axis=-1, keepdims=True) + EPS) * w


def _silu(x):
    return x * jax.nn.sigmoid(x)


def _softplus(x):
    return jnp.maximum(x, 0.0) + jnp.log(1.0 + jnp.exp(-jnp.abs(x)))


def _prefix_sum_rows(x, period):
    rows = x.shape[0]
    pos = lax.broadcasted_iota(jnp.int32, x.shape, 0) % period
    d = 1
    while d < min(period, rows):
        x = x + jnp.where(pos >= d, pltpu.roll(x, d, 0), 0.0)
        d *= 2
    return x


def _proj_ab_kernel(x_ref, pos_ref, invf_ref, nw_ref, w_ref, cw_ref, alog_ref, dtb_ref,
                    q_ref, k_ref, v_ref, az_ref, bq_ref, bk_ref, bv_ref, bz_ref, bg_ref,
                    cbuf_ref, *, tiles_per_seq):
    tm = x_ref.shape[0]
    i = pl.program_id(0)
    h = _rmsnorm_rows(x_ref[...], nw_ref[...]).astype(BF16)

    def proj(lo, width):
        return jnp.dot(h, w_ref[:, lo:lo + width], preferred_element_type=F32)

    ang = pos_ref[...].astype(F32) * invf_ref[...]
    cos = jnp.cos(ang)
    sin = jnp.sin(ang)
    d = lax.broadcasted_iota(jnp.int32, (1, LANES), 1) % SUB_DIM
    sin_hi = jnp.where((d >= ROT_HALF) & (d < ROT_DIMS), sin, 0.0)
    sin_lo = jnp.where(d < ROT_HALF, -sin, 0.0)

    def rope_store(acc, out_ref, scale):
        for g in range(N_GROUPS):
            a = acc[:, g * LANES:(g + 1) * LANES]
            r = (a * cos + pltpu.roll(a, ROT_HALF, 1) * sin_hi
                 + pltpu.roll(a, LANES - ROT_HALF, 1) * sin_lo)
            out_ref[:, g * LANES:(g + 1) * LANES] = (r * scale).astype(BF16)

    rope_store(proj(0, SEC), q_ref, Q_SCALE)
    rope_store(proj(SEC, SEC), k_ref, 1.0)
    v_ref[...] = proj(2 * SEC, SEC).astype(BF16)
    az_ref[...] = proj(3 * SEC, SEC).astype(BF16)

    @pl.when(i % tiles_per_seq == 0)
    def _():
        cbuf_ref[0:8, :] = jnp.zeros((8, 3 * SEC), F32)

    def conv_silu(sec):
        lo = sec * SEC
        cbuf_ref[8:8 + tm, lo:lo + SEC] = proj(4 * SEC + lo, SEC)
        y = jnp.zeros((tm, SEC), F32)
        for j in range(CONV_K):
            y = y + cw_ref[j:j + 1, lo:lo + SEC] * cbuf_ref[pl.ds(8 - (CONV_K - 1) + j, tm), lo:lo + SEC]
        return _silu(y)

    def l2norm_store(y, out_ref, scale):
        for g in range(N_GROUPS):
            a = y[:, g * LANES:(g + 1) * LANES]
            a = a * lax.rsqrt(jnp.sum(a * a, axis=-1, keepdims=True) + EPS)
            out_ref[:, g * LANES:(g + 1) * LANES] = (a * scale).astype(BF16)

    l2norm_store(conv_silu(0), bq_ref, LANES ** -0.5)
    l2norm_store(conv_silu(1), bk_ref, 1.0)
    bv_ref[...] = conv_silu(2).astype(BF16)
    cbuf_ref[0:8, :] = cbuf_ref[tm:tm + 8, :]

    bz_ref[...] = proj(7 * SEC, SEC).astype(BF16)

    sm = proj(8 * SEC, LANES)
    beta = jax.nn.sigmoid(sm)
    g = -jnp.exp(alog_ref[...]) * _softplus(sm + dtb_ref[...])
    lane = lax.broadcasted_iota(jnp.int32, (1, LANES), 1)
    bg_ref[...] = jnp.where(lane < GDN_HEADS, beta, g)


def _proj_ab(x2, pos2, invf, nw, w, cw, alog, dtb, *, seq):
    t, dm = x2.shape
    tm = TM_PROJ
    n_in = w.shape[1]
    row = lambda i: (i, 0)
    fix = lambda i: (0, 0)
    wide = lambda dt: jax.ShapeDtypeStruct((t, SEC), dt)
    out_shape = [wide(BF16)] * 8 + [jax.ShapeDtypeStruct((t, LANES), F32)]
    out_specs = [pl.BlockSpec((tm, SEC), row)] * 8 + [pl.BlockSpec((tm, LANES), row)]
    return pl.pallas_call(
        functools.partial(_proj_ab_kernel, tiles_per_seq=seq // tm),
        grid=(t // tm,),
        in_specs=[
            pl.BlockSpec((tm, dm), row),
            pl.BlockSpec((tm, 1), row),
            pl.BlockSpec((1, LANES), fix),
            pl.BlockSpec((1, dm), fix),
            pl.BlockSpec((dm, n_in), fix),
            pl.BlockSpec((CONV_K, 3 * SEC), fix),
            pl.BlockSpec((1, LANES), fix),
            pl.BlockSpec((1, LANES), fix),
        ],
        out_specs=out_specs,
        out_shape=out_shape,
        scratch_shapes=[pltpu.VMEM((tm + 8, 3 * SEC), F32)],
        compiler_params=pltpu.CompilerParams(
            dimension_semantics=("arbitrary",), vmem_limit_bytes=VMEM_LIMIT),
        name="proj_ab",
    )(x2, pos2, invf, nw, w, cw, alog, dtb)


def _proj_c_kernel(x_ref, nw_ref, w_ref, fb_ref, q_ref, k_ref, v_ref, z_ref, ct_ref, carry_ref,
                   *, tiles_per_seq):
    tm = x_ref.shape[0]
    i = pl.program_id(0)
    width = q_ref.shape[1]
    h = _rmsnorm_rows(x_ref[...], nw_ref[...]).astype(BF16)

    def proj(lo, n):
        return jnp.dot(h, w_ref[:, lo:lo + n], preferred_element_type=F32)

    for sec in range(width // SEC):
        lo = sec * SEC
        q_ref[:, lo:lo + SEC] = (proj(lo, SEC) * Q_SCALE).astype(BF16)
        k_ref[:, lo:lo + SEC] = proj(width + lo, SEC).astype(BF16)
        v_ref[:, lo:lo + SEC] = proj(2 * width + lo, SEC).astype(BF16)
        z_ref[:, lo:lo + SEC] = proj(3 * width + lo, SEC).astype(BF16)

    @pl.when(i % tiles_per_seq == 0)
    def _():
        carry_ref[...] = jnp.zeros_like(carry_ref)

    f = proj(4 * width, LANES) + fb_ref[...]
    log_f = -_softplus(-f)
    c = _prefix_sum_rows(log_f, tm) + carry_ref[0:1, :]
    carry_ref[...] = jnp.broadcast_to(c[tm - 1:tm, :], carry_ref.shape)
    ct_ref[0] = c.T[0:2 * FOX_PAIRS, :]


def _proj_c(x2, nw, w, fb, *, seq):
    t, dm = x2.shape
    tm = TM_PROJ
    width = 2 * FOX_PAIRS * SUB_DIM
    n_in = w.shape[1]
    tiles = seq // tm
    row = lambda i: (i, 0)
    fix = lambda i: (0, 0)
    wide = jax.ShapeDtypeStruct((t, width), BF16)
    return pl.pallas_call(
        functools.partial(_proj_c_kernel, tiles_per_seq=tiles),
        grid=(t // tm,),
        in_specs=[
            pl.BlockSpec((tm, dm), row),
            pl.BlockSpec((1, dm), fix),
            pl.BlockSpec((dm, n_in), fix),
            pl.BlockSpec((1, LANES), fix),
        ],
        out_specs=[pl.BlockSpec((tm, width), row)] * 4
        + [pl.BlockSpec((1, 2 * FOX_PAIRS, tm), lambda i: (i // tiles, 0, i % tiles))],
        out_shape=[wide] * 4
        + [jax.ShapeDtypeStruct((t // seq, 2 * FOX_PAIRS, seq), F32)],
        scratch_shapes=[pltpu.VMEM((8, LANES), F32)],
        compiler_params=pltpu.CompilerParams(
            dimension_semantics=("arbitrary",), vmem_limit_bytes=VMEM_LIMIT),
        name="proj_c",
    )(x2, nw, w, fb)


def _flash_two_softmax(q_ref, k_ref, v_ref, bias_fn, finish):
    t = TQ
    lo_lanes = lax.broadcasted_iota(jnp.int32, (1, LANES), 1) < SUB_DIM
    causal = (lax.broadcasted_iota(jnp.int32, (t, t), 1)
              <= lax.broadcasted_iota(jnp.int32, (t, t), 0))
    for i in range(q_ref.shape[1] // t):
        rows = slice(i * t, (i + 1) * t)
        q = q_ref[0, rows, :]
        zero = jnp.zeros_like(q)
        q2 = jnp.concatenate([jnp.where(lo_lanes, q, zero), jnp.where(lo_lanes, zero, q)], axis=0)
        m = jnp.full((2 * t, 1), -jnp.inf, F32)
        l = jnp.zeros((2 * t, 1), F32)
        acc = jnp.zeros((2 * t, LANES), F32)
        for j in range(i + 1):
            keys = slice(j * t, (j + 1) * t)
            s = lax.dot_general(q2, k_ref[0, keys, :], (((1,), (1,)), ((), ())),
                                preferred_element_type=F32)
            halves = [s[:t], s[t:]]
            if bias_fn is not None:
                halves = [halves[c] + bias_fn(j, c) for c in range(2)]
            if j == i:
                halves = [jnp.where(causal, h, -jnp.inf) for h in halves]
            s = jnp.concatenate(halves, axis=0)
            m_new = jnp.maximum(m, jnp.max(s, axis=1, keepdims=True))
            alpha = jnp.exp2(m - m_new)
            p = jnp.exp2(s - m_new)
            l = alpha * l + jnp.sum(p, axis=1, keepdims=True)
            acc = alpha * acc + jnp.dot(p.astype(BF16), v_ref[0, keys, :],
                                        preferred_element_type=F32)
            m = m_new
        finish(rows, lo_lanes, l, acc)


def _diff_attn_kernel(q_ref, k_ref, v_ref, z_ref, lam_ref, subln_ref, o_ref):
    t = TQ
    lv = lam_ref[...]
    lam = (jnp.exp(jnp.sum(lv[0:1] * lv[1:2], axis=-1, keepdims=True))
           - jnp.exp(jnp.sum(lv[2:3] * lv[3:4], axis=-1, keepdims=True)) + LAMBDA_INIT_0)

    def finish(rows, lo_lanes, l, acc):
        o = acc[:t] / l[:t] - lam * (acc[t:] / l[t:])
        o = _rmsnorm_rows(o, subln_ref[...]) * (1.0 - LAMBDA_INIT_0)
        o_ref[0, rows, :] = (o * _silu(z_ref[0, rows, :].astype(F32))).astype(BF16)

    _flash_two_softmax(q_ref, k_ref, v_ref, None, finish)


def _fox_attn_kernel(q_ref, k_ref, v_ref, z_ref, ct_ref, o_ref):
    t = TQ
    hp = pl.program_id(1)

    def bias(j, c):
        return ct_ref[0, j, pl.ds(2 * hp + c, 1), :] * (-LOG2E)

    def finish(rows, lo_lanes, l, acc):
        o = jnp.where(lo_lanes, acc[:t] / l[:t], acc[t:] / l[t:])
        o_ref[0, rows, :] = (o * _silu(z_ref[0, rows, :].astype(F32))).astype(BF16)

    _flash_two_softmax(q_ref, k_ref, v_ref, bias, finish)


def _attention(kernel_fn, q, k, v, z, extras, extra_specs, name):
    b, s, width = q.shape
    spec = pl.BlockSpec((1, s, LANES), lambda bi, g: (bi, 0, g))
    return pl.pallas_call(
        kernel_fn,
        grid=(b, width // LANES),
        in_specs=[spec, spec, spec, spec] + extra_specs,
        out_specs=spec,
        out_shape=jax.ShapeDtypeStruct((b, s, width), BF16),
        compiler_params=pltpu.CompilerParams(
            dimension_semantics=("arbitrary", "arbitrary"), vmem_limit_bytes=VMEM_LIMIT),
        name=name,
    )(q, k, v, z, *extras)


def _gdn_kernel(q_ref, k_ref, v_ref, z_ref, bg_ref, hn_ref, o_ref, state_ref):
    s = q_ref.shape[1]
    state_ref[...] = jnp.zeros_like(state_ref)
    ri = lax.broadcasted_iota(jnp.int32, (CHUNK, CHUNK), 0)
    ci = lax.broadcasted_iota(jnp.int32, (CHUNK, CHUNK), 1)
    tril = ci <= ri
    strict = ci < ri

    def mm(a, b):
        return jnp.dot(a.astype(BF16), b.astype(BF16), preferred_element_type=F32)

    def chunk_body(c, carry):
        r0 = pl.multiple_of(c * CHUNK, CHUNK)
        rows = pl.ds(r0, CHUNK)
        bg = bg_ref[0, rows, :]
        gcum = _prefix_sum_rows(bg, CHUNK)
        gcum_t = gcum.T
        for h in range(GDN_HEADS):
            lanes = slice(h * LANES, (h + 1) * LANES)
            beta = bg[:, h:h + 1]
            gcol = gcum[:, GDN_HEADS + h:GDN_HEADS + h + 1]
            grow = gcum_t[GDN_HEADS + h:GDN_HEADS + h + 1, :]
            glast = gcol[CHUNK - 1:CHUNK, :]
            q = q_ref[0, rows, lanes]
            k = k_ref[0, rows, lanes]
            kf = k.astype(F32)
            vf = v_ref[0, rows, lanes].astype(F32)
            k_beta = kf * beta
            decay = jnp.exp(jnp.where(tril, gcol - grow, -jnp.inf))
            qk_kk = lax.dot_general(
                jnp.concatenate([q, k_beta.astype(BF16)], axis=0), k,
                (((1,), (1,)), ((), ())), preferred_element_type=F32)
            qk = jnp.where(tril, qk_kk[:CHUNK] * decay, 0.0)
            low = jnp.where(strict, qk_kk[CHUNK:] * decay, 0.0)
            x = jnp.concatenate([vf * beta, k_beta * jnp.exp(gcol)], axis=1)
            x = x - mm(low, x)
            p = low
            for _ in range(5):
                p = mm(p, p)
                x = x + mm(p, x)
            u = x[:, :LANES]
            w = x[:, LANES:]
            st = state_ref[h]
            v_new = u - mm(w, st)
            o = mm(q.astype(F32) * jnp.exp(gcol), st) + mm(qk, v_new)
            k_dec = (kf * jnp.exp(glast - gcol)).astype(BF16)
            state_ref[h] = st * jnp.exp(glast) + lax.dot_general(
                k_dec, v_new.astype(BF16), (((0,), (0,)), ((), ())), preferred_element_type=F32)
            o = _rmsnorm_rows(o, hn_ref[...])
            o_ref[0, rows, lanes] = (o * _silu(z_ref[0, rows, lanes].astype(F32))).astype(BF16)
        return carry

    lax.fori_loop(0, s // CHUNK, chunk_body, 0)


def _gdn(q, k, v, z, bg, hn):
    b, s, width = q.shape
    spec = pl.BlockSpec((1, s, width), lambda bi: (bi, 0, 0))
    return pl.pallas_call(
        _gdn_kernel,
        grid=(b,),
        in_specs=[spec, spec, spec, spec,
                  pl.BlockSpec((1, s, LANES), lambda bi: (bi, 0, 0)),
                  pl.BlockSpec((1, LANES), lambda bi: (0, 0))],
        out_specs=spec,
        out_shape=jax.ShapeDtypeStruct((b, s, width), BF16),
        scratch_shapes=[pltpu.VMEM((GDN_HEADS, LANES, LANES), F32)],
        compiler_params=pltpu.CompilerParams(
            dimension_semantics=("arbitrary",), vmem_limit_bytes=VMEM_LIMIT),
        name="gdn",
    )(q, k, v, z, bg, hn)


def _out_proj_kernel(*refs, n_parts):
    x_ref = refs[0]
    parts = refs[1:1 + n_parts]
    w_ref, pw_ref, o_ref = refs[1 + n_parts:]
    y = None
    lo = 0
    for part in parts:
        n = part.shape[1]
        d = jnp.dot(part[...], w_ref[lo:lo + n, :], preferred_element_type=F32)
        y = d if y is None else y + d
        lo += n
    o_ref[...] = x_ref[...] + _rmsnorm_rows(y, pw_ref[...])


def _out_proj(x2, parts, w, pw, name):
    t, dm = x2.shape
    tm = TM_OUT
    row = lambda i: (i, 0)
    fix = lambda i: (0, 0)
    return pl.pallas_call(
        functools.partial(_out_proj_kernel, n_parts=len(parts)),
        grid=(t // tm,),
        in_specs=[pl.BlockSpec((tm, dm), row)]
        + [pl.BlockSpec((tm, p.shape[1]), row) for p in parts]
        + [pl.BlockSpec(w.shape, fix), pl.BlockSpec((1, dm), fix)],
        out_specs=pl.BlockSpec((tm, dm), row),
        out_shape=jax.ShapeDtypeStruct((t, dm), F32),
        compiler_params=pltpu.CompilerParams(
            dimension_semantics=("arbitrary",), vmem_limit_bytes=VMEM_LIMIT),
        name=name,
    )(x2, *parts, w, pw)


def _pad_lanes(vec, offset=0):
    out = jnp.zeros((1, LANES), F32)
    return out.at[0, offset:offset + vec.shape[0]].set(vec.astype(F32))


def kernel(x, positions, pre_norm, post_norm, w_in_ab, a_lambda_q1, a_lambda_k1, a_lambda_q2,
           a_lambda_k2, a_subln, b_conv_w, b_a_log, b_dt_bias, b_head_norm, w_out_ab,
           w_in_c, c_forget_bias, w_out_c):
    b, s, dm = x.shape
    t = b * s
    x2 = x.reshape(t, dm)

    main = 8 * SEC
    w_ab = jnp.concatenate(
        [w_in_ab[:, :main], jnp.pad(w_in_ab[:, main:], ((0, 0), (0, LANES - 2 * GDN_HEADS)))],
        axis=1).astype(BF16)
    inv_freq = ROPE_THETA ** (-(jnp.arange(ROT_HALF, dtype=F32) * 2.0) / ROT_DIMS)
    d = jnp.arange(LANES) % SUB_DIM
    invf = jnp.where(d < ROT_DIMS, inv_freq[d % ROT_HALF], 0.0).reshape(1, LANES)
    (a_q, a_k, a_v, a_z, b_q, b_k, b_v, b_z, b_bg) = _proj_ab(
        x2, positions.reshape(t, 1), invf, pre_norm[0:1], w_ab, b_conv_w.astype(F32),
        _pad_lanes(b_a_log, GDN_HEADS), _pad_lanes(b_dt_bias, GDN_HEADS), seq=s)

    lam_vecs = jnp.concatenate(
        [_pad_lanes(v) for v in (a_lambda_q1, a_lambda_k1, a_lambda_q2, a_lambda_k2)], axis=0)
    r3 = lambda a: a.reshape(b, s, a.shape[-1])
    o_a = _attention(
        _diff_attn_kernel, r3(a_q), r3(a_k), r3(a_v), r3(a_z),
        [lam_vecs, a_subln.astype(F32).reshape(1, LANES)],
        [pl.BlockSpec((4, LANES), lambda bi, g: (0, 0)),
         pl.BlockSpec((1, LANES), lambda bi, g: (0, 0))],
        "diff_attn")
    o_b = _gdn(r3(b_q), r3(b_k), r3(b_v), r3(b_z), r3(b_bg),
               b_head_norm.astype(F32).reshape(1, LANES))
    x2 = _out_proj(x2, [o_a.reshape(t, SEC), o_b.reshape(t, SEC)], w_out_ab.astype(BF16),
                   post_norm[0:1], "out_proj_ab")

    width = 2 * FOX_PAIRS * SUB_DIM
    w_c = jnp.concatenate(
        [w_in_c[:, :4 * width],
         jnp.pad(w_in_c[:, 4 * width:], ((0, 0), (0, LANES - 2 * FOX_PAIRS)))], axis=1).astype(BF16)
    c_q, c_k, c_v, c_z, c_t = _proj_c(x2, pre_norm[1:2], w_c, _pad_lanes(c_forget_bias), seq=s)
    c_t = c_t.reshape(b, 2 * FOX_PAIRS, s // TQ, TQ).transpose(0, 2, 1, 3)
    o_c = _attention(
        _fox_attn_kernel, r3(c_q), r3(c_k), r3(c_v), r3(c_z), [c_t],
        [pl.BlockSpec((1, s // TQ, 2 * FOX_PAIRS, TQ), lambda bi, g: (bi, 0, 0, 0))],
        "fox_attn")
    x2 = _out_proj(x2, [o_c.reshape(t, width)], w_out_c.astype(BF16), post_norm[1:2],
                   "out_proj_c")
    return x2.reshape(b, s, dm)
```

```python
import functools
import math

import jax
import jax.numpy as jnp
from jax import lax
from jax.experimental import pallas as pl
from jax.experimental.pallas import tpu as pltpu

F32 = jnp.float32
BF16 = jnp.bfloat16

EPS = 1e-6
ROPE_THETA = 500000.0
ROT_DIMS = 16
ROT_HALF = ROT_DIMS // 2
SUB_DIM = 64
LANES = 128
N_GROUPS = 4
SEC = N_GROUPS * LANES
CONV_K = 4
CHUNK = 64
GDN_HEADS = 4
FOX_PAIRS = 8
LAMBDA_INIT_0 = 0.8 - 0.6 * math.exp(-0.3 * 0)
VMEM_LIMIT = 48 * 1024 * 1024
GDN_VMEM_LIMIT = 56 * 1024 * 1024

TM_PROJ = 256
TM_OUT = 512
TQ = 512
LOG2E = math.log2(math.e)
Q_SCALE = SUB_DIM ** -0.5 * LOG2E


def _rmsnorm_rows(x, w):
    return x * lax.rsqrt(jnp.mean(x * x, axis=-1, keepdims=True) + EPS) * w


def _silu(x):
    return x * jax.nn.sigmoid(x)


def _softplus(x):
    return jnp.maximum(x, 0.0) + jnp.log(1.0 + jnp.exp(-jnp.abs(x)))


def _prefix_sum_rows(x, period):
    rows = x.shape[0]
    pos = lax.broadcasted_iota(jnp.int32, x.shape, 0) % period
    d = 1
    while d < min(period, rows):
        x = x + jnp.where(pos >= d, pltpu.roll(x, d, 0), 0.0)
        d *= 2
    return x


def _proj_ab_kernel(x_ref, pos_ref, invf_ref, nw_ref, w_ref, cw_ref, alog_ref, dtb_ref,
                    q_ref, k_ref, v_ref, az_ref, bq_ref, bk_ref, bv_ref, bz_ref, bg_ref,
                    cbuf_ref, *, tiles_per_seq):
    tm = x_ref.shape[0]
    i = pl.program_id(0)
    h = _rmsnorm_rows(x_ref[...], nw_ref[...]).astype(BF16)

    def proj(lo, width):
        return jnp.dot(h, w_ref[:, lo:lo + width], preferred_element_type=F32)

    ang = pos_ref[...].astype(F32) * invf_ref[...]
    cos = jnp.cos(ang)
    sin = jnp.sin(ang)
    d = lax.broadcasted_iota(jnp.int32, (1, LANES), 1) % SUB_DIM
    sin_hi = jnp.where((d >= ROT_HALF) & (d < ROT_DIMS), sin, 0.0)
    sin_lo = jnp.where(d < ROT_HALF, -sin, 0.0)

    def rope_store(acc, out_ref, scale):
        for g in range(N_GROUPS):
            a = acc[:, g * LANES:(g + 1) * LANES]
            r = (a * cos + pltpu.roll(a, ROT_HALF, 1) * sin_hi
                 + pltpu.roll(a, LANES - ROT_HALF, 1) * sin_lo)
            out_ref[:, g * LANES:(g + 1) * LANES] = (r * scale).astype(BF16)

    rope_store(proj(0, SEC), q_ref, Q_SCALE)
    rope_store(proj(SEC, SEC), k_ref, 1.0)
    v_ref[...] = proj(2 * SEC, SEC).astype(BF16)
    az_ref[...] = proj(3 * SEC, SEC).astype(BF16)

    @pl.when(i % tiles_per_seq == 0)
    def _():
        cbuf_ref[0:8, :] = jnp.zeros((8, 3 * SEC), F32)

    def conv_silu(sec):
        lo = sec * SEC
        cbuf_ref[8:8 + tm, lo:lo + SEC] = proj(4 * SEC + lo, SEC)
        y = jnp.zeros((tm, SEC), F32)
        for j in range(CONV_K):
            y = y + cw_ref[j:j + 1, lo:lo + SEC] * cbuf_ref[pl.ds(8 - (CONV_K - 1) + j, tm), lo:lo + SEC]
        return _silu(y)

    def l2norm_store(y, out_ref, scale):
        for g in range(N_GROUPS):
            a = y[:, g * LANES:(g + 1) * LANES]
            a = a * lax.rsqrt(jnp.sum(a * a, axis=-1, keepdims=True) + EPS)
            out_ref[:, g * LANES:(g + 1) * LANES] = (a * scale).astype(BF16)

    l2norm_store(conv_silu(0), bq_ref, LANES ** -0.5)
    l2norm_store(conv_silu(1), bk_ref, 1.0)
    bv_ref[...] = conv_silu(2).astype(BF16)
    cbuf_ref[0:8, :] = cbuf_ref[tm:tm + 8, :]

    bz_ref[...] = proj(7 * SEC, SEC).astype(BF16)

    sm = proj(8 * SEC, LANES)
    beta = jax.nn.sigmoid(sm)
    g = -jnp.exp(alog_ref[...]) * _softplus(sm + dtb_ref[...])
    lane = lax.broadcasted_iota(jnp.int32, (1, LANES), 1)
    bg_ref[...] = jnp.where(lane < GDN_HEADS, beta, g)


def _proj_ab(x2, pos2, invf, nw, w, cw, alog, dtb, *, seq):
    t, dm = x2.shape
    tm = TM_PROJ
    n_in = w.shape[1]
    row = lambda i: (i, 0)
    fix = lambda i: (0, 0)
    wide = lambda dt: jax.ShapeDtypeStruct((t, SEC), dt)
    out_shape = [wide(BF16)] * 8 + [jax.ShapeDtypeStruct((t, LANES), F32)]
    out_specs = [pl.BlockSpec((tm, SEC), row)] * 8 + [pl.BlockSpec((tm, LANES), row)]
    return pl.pallas_call(
        functools.partial(_proj_ab_kernel, tiles_per_seq=seq // tm),
        grid=(t // tm,),
        in_specs=[
            pl.BlockSpec((tm, dm), row),
            pl.BlockSpec((tm, 1), row),
            pl.BlockSpec((1, LANES), fix),
            pl.BlockSpec((1, dm), fix),
            pl.BlockSpec((dm, n_in), fix),
            pl.BlockSpec((CONV_K, 3 * SEC), fix),
            pl.BlockSpec((1, LANES), fix),
            pl.BlockSpec((1, LANES), fix),
        ],
        out_specs=out_specs,
        out_shape=out_shape,
        scratch_shapes=[pltpu.VMEM((tm + 8, 3 * SEC), F32)],
        compiler_params=pltpu.CompilerParams(
            dimension_semantics=("arbitrary",), vmem_limit_bytes=VMEM_LIMIT),
        name="proj_ab",
    )(x2, pos2, invf, nw, w, cw, alog, dtb)


def _proj_c_kernel(x_ref, nw_ref, w_ref, fb_ref, q_ref, k_ref, v_ref, z_ref, ct_ref, carry_ref,
                   *, tiles_per_seq):
    tm = x_ref.shape[0]
    i = pl.program_id(0)
    width = q_ref.shape[1]
    h = _rmsnorm_rows(x_ref[...], nw_ref[...]).astype(BF16)

    def proj(lo, n):
        return jnp.dot(h, w_ref[:, lo:lo + n], preferred_element_type=F32)

    for sec in range(width // SEC):
        lo = sec * SEC
        q_ref[:, lo:lo + SEC] = (proj(lo, SEC) * Q_SCALE).astype(BF16)
        k_ref[:, lo:lo + SEC] = proj(width + lo, SEC).astype(BF16)
        v_ref[:, lo:lo + SEC] = proj(2 * width + lo, SEC).astype(BF16)
        z_ref[:, lo:lo + SEC] = proj(3 * width + lo, SEC).astype(BF16)

    @pl.when(i % tiles_per_seq == 0)
    def _():
        carry_ref[...] = jnp.zeros_like(carry_ref)

    f = proj(4 * width, LANES) + fb_ref[...]
    log_f = -_softplus(-f)
    c = _prefix_sum_rows(log_f, tm) + carry_ref[0:1, :]
    carry_ref[...] = jnp.broadcast_to(c[tm - 1:tm, :], carry_ref.shape)
    ct_ref[0] = c.T[0:2 * FOX_PAIRS, :]


def _proj_c(x2, nw, w, fb, *, seq):
    t, dm = x2.shape
    tm = TM_PROJ
    width = 2 * FOX_PAIRS * SUB_DIM
    n_in = w.shape[1]
    tiles = seq // tm
    row = lambda i: (i, 0)
    fix = lambda i: (0, 0)
    wide = jax.ShapeDtypeStruct((t, width), BF16)
    return pl.pallas_call(
        functools.partial(_proj_c_kernel, tiles_per_seq=tiles),
        grid=(t // tm,),
        in_specs=[
            pl.BlockSpec((tm, dm), row),
            pl.BlockSpec((1, dm), fix),
            pl.BlockSpec((dm, n_in), fix),
            pl.BlockSpec((1, LANES), fix),
        ],
        out_specs=[pl.BlockSpec((tm, width), row)] * 4
        + [pl.BlockSpec((1, 2 * FOX_PAIRS, tm), lambda i: (i // tiles, 0, i % tiles))],
        out_shape=[wide] * 4
        + [jax.ShapeDtypeStruct((t // seq, 2 * FOX_PAIRS, seq), F32)],
        scratch_shapes=[pltpu.VMEM((8, LANES), F32)],
        compiler_params=pltpu.CompilerParams(
            dimension_semantics=("arbitrary",), vmem_limit_bytes=VMEM_LIMIT),
        name="proj_c",
    )(x2, nw, w, fb)


def _flash_two_softmax(q_ref, k_ref, v_ref, bias_fn, finish):
    t = TQ
    lo_lanes = lax.broadcasted_iota(jnp.int32, (1, LANES), 1) < SUB_DIM
    causal = (lax.broadcasted_iota(jnp.int32, (t, t), 1)
              <= lax.broadcasted_iota(jnp.int32, (t, t), 0))
    for i in range(q_ref.shape[1] // t):
        rows = slice(i * t, (i + 1) * t)
        q = q_ref[0, rows, :]
        zero = jnp.zeros_like(q)
        q2 = jnp.concatenate([jnp.where(lo_lanes, q, zero), jnp.where(lo_lanes, zero, q)], axis=0)
        m = jnp.full((2 * t, 1), -jnp.inf, F32)
        l = jnp.zeros((2 * t, 1), F32)
        acc = jnp.zeros((2 * t, LANES), F32)
        for j in range(i + 1):
            keys = slice(j * t, (j + 1) * t)
            s = lax.dot_general(q2, k_ref[0, keys, :], (((1,), (1,)), ((), ())),
                                preferred_element_type=F32)
            halves = [s[:t], s[t:]]
            if bias_fn is not None:
                halves = [halves[c] + bias_fn(j, c) for c in range(2)]
            if j == i:
                halves = [jnp.where(causal, h, -jnp.inf) for h in halves]
            s = jnp.concatenate(halves, axis=0)
            m_new = jnp.maximum(m, jnp.max(s, axis=1, keepdims=True))
            alpha = jnp.exp2(m - m_new)
            p = jnp.exp2(s - m_new)
            l = alpha * l + jnp.sum(p, axis=1, keepdims=True)
            acc = alpha * acc + jnp.dot(p.astype(BF16), v_ref[0, keys, :],
                                        preferred_element_type=F32)
            m = m_new
        finish(rows, lo_lanes, l, acc)


def _diff_attn_kernel(q_ref, k_ref, v_ref, z_ref, lam_ref, subln_ref, o_ref):
    t = TQ
    lv = lam_ref[...]
    lam = (jnp.exp(jnp.sum(lv[0:1] * lv[1:2], axis=-1, keepdims=True))
           - jnp.exp(jnp.sum(lv[2:3] * lv[3:4], axis=-1, keepdims=True)) + LAMBDA_INIT_0)

    def finish(rows, lo_lanes, l, acc):
        o = acc[:t] / l[:t] - lam * (acc[t:] / l[t:])
        o = _rmsnorm_rows(o, subln_ref[...]) * (1.0 - LAMBDA_INIT_0)
        o_ref[0, rows, :] = (o * _silu(z_ref[0, rows, :].astype(F32))).astype(BF16)

    _flash_two_softmax(q_ref, k_ref, v_ref, None, finish)


def _fox_attn_kernel(q_ref, k_ref, v_ref, z_ref, ct_ref, o_ref):
    t = TQ
    hp = pl.program_id(1)

    def bias(j, c):
        return ct_ref[0, j, pl.ds(2 * hp + c, 1), :] * (-LOG2E)

    def finish(rows, lo_lanes, l, acc):
        o = jnp.where(lo_lanes, acc[:t] / l[:t], acc[t:] / l[t:])
        o_ref[0, rows, :] = (o * _silu(z_ref[0, rows, :].astype(F32))).astype(BF16)

    _flash_two_softmax(q_ref, k_ref, v_ref, bias, finish)


def _attention(kernel_fn, q, k, v, z, extras, extra_specs, name):
    b, s, width = q.shape
    spec = pl.BlockSpec((1, s, LANES), lambda bi, g: (bi, 0, g))
    return pl.pallas_call(
        kernel_fn,
        grid=(b, width // LANES),
        in_specs=[spec, spec, spec, spec] + extra_specs,
        out_specs=spec,
        out_shape=jax.ShapeDtypeStruct((b, s, width), BF16),
        compiler_params=pltpu.CompilerParams(
            dimension_semantics=("arbitrary", "arbitrary"), vmem_limit_bytes=VMEM_LIMIT),
        name=name,
    )(q, k, v, z, *extras)


def _gdn_kernel(q_ref, k_ref, v_ref, z_ref, bg_ref, hn_ref, o_ref,
                state_ref, gc_ref, low_ref, qk_ref, u_ref, w_ref, qg_ref, kdt_ref, gam_ref):
    s = q_ref.shape[1]
    n_chunks = s // CHUNK
    ri = lax.broadcasted_iota(jnp.int32, (CHUNK, CHUNK), 0)
    ci = lax.broadcasted_iota(jnp.int32, (CHUNK, CHUNK), 1)
    tril = ci <= ri
    strict = ci < ri

    def mm(a, b):
        return jnp.dot(a.astype(BF16), b.astype(BF16), preferred_element_type=F32)

    def chunk_rows(c):
        return pl.ds(pl.multiple_of(c * CHUNK, CHUNK), CHUNK)

    def head_lanes(h):
        return slice(h * LANES, (h + 1) * LANES)

    def phase_a(c, carry):
        rows = chunk_rows(c)
        bg = bg_ref[0, rows, :]
        gcum = _prefix_sum_rows(bg, CHUNK)
        gc_ref[rows, :] = gcum
        gcum_t = gcum.T
        prods = []
        for h in range(GDN_HEADS):
            q = q_ref[0, rows, head_lanes(h)]
            k = k_ref[0, rows, head_lanes(h)]
            k_beta = (k.astype(F32) * bg[:, h:h + 1]).astype(BF16)
            prods.append(lax.dot_general(
                jnp.concatenate([q, k_beta], axis=0), k,
                (((1,), (1,)), ((), ())), preferred_element_type=F32))
        for h in range(GDN_HEADS):
            n = c * GDN_HEADS + h
            gcol = gcum[:, GDN_HEADS + h:GDN_HEADS + h + 1]
            grow = gcum_t[GDN_HEADS + h:GDN_HEADS + h + 1, :]
            decay = jnp.exp(jnp.where(tril, gcol - grow, -jnp.inf))
            qk_ref[n] = jnp.where(tril, prods[h][:CHUNK] * decay, 0.0).astype(BF16)
            low = jnp.where(strict, prods[h][CHUNK:] * decay, 0.0)
            low_ref[n] = jnp.concatenate([low, jnp.zeros_like(low)], axis=1)
        return carry

    lax.fori_loop(0, n_chunks, phase_a, 0, unroll=2)

    rowl_ref = u_ref.at[0:CHUNK]
    tinv_ref = u_ref.at[CHUNK:2 * CHUNK]
    for i in range(CHUNK):
        rowl_ref[i] = low_ref[:, i, :].T[0:CHUNK, :]
    sub = lax.broadcasted_iota(jnp.int32, (8, LANES), 0)
    for i in range(CHUNK):
        groups = i // 8 + 1
        acc = [[None, None] for _ in range(groups)]
        for j in range(i):
            coef = rowl_ref[i, j:j + 1, :]
            for g in range(j // 8 + 1):
                term = coef * tinv_ref[j, 8 * g:8 * g + 8, :]
                slot = acc[g]
                slot[j % 2] = term if slot[j % 2] is None else slot[j % 2] + term
        for g in range(CHUNK // 8):
            if g < groups:
                val = jnp.where(sub == i % 8, 1.0, 0.0) if g == groups - 1 else jnp.zeros((8, LANES), F32)
                for part in acc[g]:
                    if part is not None:
                        val = val - part
            else:
                val = jnp.zeros((8, LANES), F32)
            tinv_ref[i, 8 * g:8 * g + 8, :] = val
    for i in range(CHUNK):
        t_i = jnp.concatenate([tinv_ref[i], jnp.zeros((LANES - CHUNK, LANES), F32)], axis=0)
        low_ref[:, i, :] = t_i.T

    def phase_c(c, carry):
        rows = chunk_rows(c)
        bg = bg_ref[0, rows, :]
        gcum = gc_ref[rows, :]
        heads = range(GDN_HEADS)
        gcols = [gcum[:, GDN_HEADS + h:GDN_HEADS + h + 1] for h in heads]
        e_gs = [jnp.exp(g) for g in gcols]
        kfs = [k_ref[0, rows, head_lanes(h)].astype(F32) for h in heads]
        xs = []
        for h in heads:
            beta = bg[:, h:h + 1]
            v_beta = v_ref[0, rows, head_lanes(h)].astype(F32) * beta
            rhs = jnp.concatenate([v_beta, kfs[h] * beta * e_gs[h]], axis=1)
            xs.append(mm(low_ref[c * GDN_HEADS + h][:, 0:CHUNK], rhs))
        for h in heads:
            n = c * GDN_HEADS + h
            glast = gcols[h][CHUNK - 1:CHUNK, :]
            u_ref[n] = xs[h][:, :LANES]
            w_ref[n] = xs[h][:, LANES:].astype(BF16)
            qg_ref[n] = (q_ref[0, rows, head_lanes(h)].astype(F32) * e_gs[h]).astype(BF16)
            kdt_ref[n] = (kfs[h] * jnp.exp(glast - gcols[h])).T.astype(BF16)
            gam_ref[n] = jnp.broadcast_to(jnp.exp(glast), (1, LANES))
        return carry

    lax.fori_loop(0, n_chunks, phase_c, 0, unroll=2)

    state_ref[...] = jnp.zeros_like(state_ref)

    def phase_d(c, carry):
        rows = chunk_rows(c)
        heads = range(GDN_HEADS)
        ns = [c * GDN_HEADS + h for h in heads]
        dot = functools.partial(jnp.dot, preferred_element_type=F32)
        sts = [state_ref[h] for h in heads]
        st_bs = [st.astype(BF16) for st in sts]
        w_st = [dot(w_ref[ns[h]], st_bs[h]) for h in heads]
        q_st = [dot(qg_ref[ns[h]], st_bs[h]) for h in heads]
        v_bs = [(u_ref[ns[h]] - w_st[h]).astype(BF16) for h in heads]
        k_v = [dot(kdt_ref[ns[h]], v_bs[h]) for h in heads]
        qk_v = [dot(qk_ref[ns[h]], v_bs[h]) for h in heads]
        for h in heads:
            state_ref[h] = sts[h] * gam_ref[ns[h]] + k_v[h]
        for h in heads:
            o = _rmsnorm_rows(q_st[h] + qk_v[h], hn_ref[...])
            gate = _silu(z_ref[0, rows, head_lanes(h)].astype(F32))
            o_ref[0, rows, head_lanes(h)] = (o * gate).astype(BF16)
        return carry

    lax.fori_loop(0, n_chunks, phase_d, 0, unroll=2)


def _gdn(q, k, v, z, bg, hn):
    b, s, width = q.shape
    n_sys = s // CHUNK * GDN_HEADS
    assert n_sys == LANES, "the lane-batched triangular inverse needs exactly 128 (chunk, head) systems"
    spec = pl.BlockSpec((1, s, width), lambda bi: (bi, 0, 0))
    return pl.pallas_call(
        _gdn_kernel,
        grid=(b,),
        in_specs=[spec, spec, spec, spec,
                  pl.BlockSpec((1, s, LANES), lambda bi: (bi, 0, 0)),
                  pl.BlockSpec((1, LANES), lambda bi: (0, 0))],
        out_specs=spec,
        out_shape=jax.ShapeDtypeStruct((b, s, width), BF16),
        scratch_shapes=[
            pltpu.VMEM((GDN_HEADS, LANES, LANES), F32),
            pltpu.VMEM((s, LANES), F32),
            pltpu.VMEM((n_sys, CHUNK, LANES), F32),
            pltpu.VMEM((n_sys, CHUNK, CHUNK), BF16),
            pltpu.VMEM((n_sys, CHUNK, LANES), F32),
            pltpu.VMEM((n_sys, CHUNK, LANES), BF16),
            pltpu.VMEM((n_sys, CHUNK, LANES), BF16),
            pltpu.VMEM((n_sys, LANES, CHUNK), BF16),
            pltpu.VMEM((n_sys, 1, LANES), F32),
        ],
        compiler_params=pltpu.CompilerParams(
            dimension_semantics=("arbitrary",), vmem_limit_bytes=GDN_VMEM_LIMIT),
        name="gdn",
    )(q, k, v, z, bg, hn)


def _out_proj_kernel(*refs, n_parts):
    x_ref = refs[0]
    parts = refs[1:1 + n_parts]
    w_ref, pw_ref, o_ref = refs[1 + n_parts:]
    y = None
    lo = 0
    for part in parts:
        n = part.shape[1]
        d = jnp.dot(part[...], w_ref[lo:lo + n, :], preferred_element_type=F32)
        y = d if y is None else y + d
        lo += n
    o_ref[...] = x_ref[...] + _rmsnorm_rows(y, pw_ref[...])


def _out_proj(x2, parts, w, pw, name):
    t, dm = x2.shape
    tm = TM_OUT
    row = lambda i: (i, 0)
    fix = lambda i: (0, 0)
    return pl.pallas_call(
        functools.partial(_out_proj_kernel, n_parts=len(parts)),
        grid=(t // tm,),
        in_specs=[pl.BlockSpec((tm, dm), row)]
        + [pl.BlockSpec((tm, p.shape[1]), row) for p in parts]
        + [pl.BlockSpec(w.shape, fix), pl.BlockSpec((1, dm), fix)],
        out_specs=pl.BlockSpec((tm, dm), row),
        out_shape=jax.ShapeDtypeStruct((t, dm), F32),
        compiler_params=pltpu.CompilerParams(
            dimension_semantics=("arbitrary",), vmem_limit_bytes=VMEM_LIMIT),
        name=name,
    )(x2, *parts, w, pw)


def _pad_lanes(vec, offset=0):
    out = jnp.zeros((1, LANES), F32)
    return out.at[0, offset:offset + vec.shape[0]].set(vec.astype(F32))


def kernel(x, positions, pre_norm, post_norm, w_in_ab, a_lambda_q1, a_lambda_k1, a_lambda_q2,
           a_lambda_k2, a_subln, b_conv_w, b_a_log, b_dt_bias, b_head_norm, w_out_ab,
           w_in_c, c_forget_bias, w_out_c):
    b, s, dm = x.shape
    t = b * s
    x2 = x.reshape(t, dm)

    main = 8 * SEC
    w_ab = jnp.concatenate(
        [w_in_ab[:, :main], jnp.pad(w_in_ab[:, main:], ((0, 0), (0, LANES - 2 * GDN_HEADS)))],
        axis=1).astype(BF16)
    inv_freq = ROPE_THETA ** (-(jnp.arange(ROT_HALF, dtype=F32) * 2.0) / ROT_DIMS)
    d = jnp.arange(LANES) % SUB_DIM
    invf = jnp.where(d < ROT_DIMS, inv_freq[d % ROT_HALF], 0.0).reshape(1, LANES)
    (a_q, a_k, a_v, a_z, b_q, b_k, b_v, b_z, b_bg) = _proj_ab(
        x2, positions.reshape(t, 1), invf, pre_norm[0:1], w_ab, b_conv_w.astype(F32),
        _pad_lanes(b_a_log, GDN_HEADS), _pad_lanes(b_dt_bias, GDN_HEADS), seq=s)

    lam_vecs = jnp.concatenate(
        [_pad_lanes(v) for v in (a_lambda_q1, a_lambda_k1, a_lambda_q2, a_lambda_k2)], axis=0)
    r3 = lambda a: a.reshape(b, s, a.shape[-1])
    o_a = _attention(
        _diff_attn_kernel, r3(a_q), r3(a_k), r3(a_v), r3(a_z),
        [lam_vecs, a_subln.astype(F32).reshape(1, LANES)],
        [pl.BlockSpec((4, LANES), lambda bi, g: (0, 0)),
         pl.BlockSpec((1, LANES), lambda bi, g: (0, 0))],
        "diff_attn")
    o_b = _gdn(r3(b_q), r3(b_k), r3(b_v), r3(b_z), r3(b_bg),
               b_head_norm.astype(F32).reshape(1, LANES))
    x2 = _out_proj(x2, [o_a.reshape(t, SEC), o_b.reshape(t, SEC)], w_out_ab.astype(BF16),
                   post_norm[0:1], "out_proj_ab")

    width = 2 * FOX_PAIRS * SUB_DIM
    w_c = jnp.concatenate(
        [w_in_c[:, :4 * width],
         jnp.pad(w_in_c[:, 4 * width:], ((0, 0), (0, LANES - 2 * FOX_PAIRS)))], axis=1).astype(BF16)
    c_q, c_k, c_v, c_z, c_t = _proj_c(x2, pre_norm[1:2], w_c, _pad_lanes(c_forget_bias), seq=s)
    c_t = c_t.reshape(b, 2 * FOX_PAIRS, s // TQ, TQ).transpose(0, 2, 1, 3)
    o_c = _attention(
        _fox_attn_kernel, r3(c_q), r3(c_k), r3(c_v), r3(c_z), [c_t],
        [pl.BlockSpec((1, s // TQ, 2 * FOX_PAIRS, TQ), lambda bi, g: (bi, 0, 0, 0))],
        "fox_attn")
    x2 = _out_proj(x2, [o_c.reshape(t, width)], w_out_c.astype(BF16), post_norm[1:2],
                   "out_proj_c")
    return x2.reshape(b, s, dm)
```

```python
import functools
import math

import jax
import jax.numpy as jnp
from jax import lax
from jax.experimental import pallas as pl
from jax.experimental.pallas import tpu as pltpu

F32 = jnp.float32
BF16 = jnp.bfloat16

EPS = 1e-6
ROPE_THETA = 500000.0
ROT_DIMS = 16
ROT_HALF = ROT_DIMS // 2
SUB_DIM = 64
LANES = 128
N_GROUPS = 4
SEC = N_GROUPS * LANES
CONV_K = 4
CHUNK = 64
GDN_HEADS = 4
GDN_UNROLL = 8
FOX_PAIRS = 8
LAMBDA_INIT_0 = 0.8 - 0.6 * math.exp(-0.3 * 0)
VMEM_LIMIT = 48 * 1024 * 1024
GDN_VMEM_LIMIT = 56 * 1024 * 1024

TM_PROJ = 256
TM_OUT = 1024
TQ = 512
LOG2E = math.log2(math.e)
Q_SCALE = SUB_DIM ** -0.5 * LOG2E


def _rmsnorm_rows(x, w):
    return x * lax.rsqrt(jnp.mean(x * x, axis=-1, keepdims=True) + EPS) * w


def _silu(x):
    return x * jax.nn.sigmoid(x)


def _softplus(x):
    return jnp.maximum(x, 0.0) + jnp.log(1.0 + jnp.exp(-jnp.abs(x)))


def _prefix_sum_rows(x, period):
    rows = x.shape[0]
    pos = lax.broadcasted_iota(jnp.int32, x.shape, 0) % period
    d = 1
    while d < min(period, rows):
        x = x + jnp.where(pos >= d, pltpu.roll(x, d, 0), 0.0)
        d *= 2
    return x


def _proj_ab_kernel(x_ref, pos_ref, invf_ref, nw_ref, w_ref, cw_ref, alog_ref, dtb_ref,
                    q_ref, k_ref, v_ref, az_ref, bq_ref, bk_ref, bv_ref, bz_ref, bg_ref,
                    cbuf_ref, *, tiles_per_seq):
    tm = x_ref.shape[0]
    i = pl.program_id(0)
    h = _rmsnorm_rows(x_ref[...], nw_ref[...]).astype(BF16)

    def proj(lo, width):
        return jnp.dot(h, w_ref[:, lo:lo + width], preferred_element_type=F32)

    ang = pos_ref[...].astype(F32) * invf_ref[...]
    cos = jnp.cos(ang)
    sin = jnp.sin(ang)
    d = lax.broadcasted_iota(jnp.int32, (1, LANES), 1) % SUB_DIM
    sin_hi = jnp.where((d >= ROT_HALF) & (d < ROT_DIMS), sin, 0.0)
    sin_lo = jnp.where(d < ROT_HALF, -sin, 0.0)

    def rope_store(acc, out_ref, scale):
        for g in range(N_GROUPS):
            a = acc[:, g * LANES:(g + 1) * LANES]
            r = (a * cos + pltpu.roll(a, ROT_HALF, 1) * sin_hi
                 + pltpu.roll(a, LANES - ROT_HALF, 1) * sin_lo)
            out_ref[:, g * LANES:(g + 1) * LANES] = (r * scale).astype(BF16)

    rope_store(proj(0, SEC), q_ref, Q_SCALE)
    rope_store(proj(SEC, SEC), k_ref, 1.0)
    v_ref[...] = proj(2 * SEC, SEC).astype(BF16)
    az_ref[...] = proj(3 * SEC, SEC).astype(BF16)

    @pl.when(i % tiles_per_seq == 0)
    def _():
        cbuf_ref[...] = jnp.zeros_like(cbuf_ref)

    row8 = lax.broadcasted_iota(jnp.int32, (8, SEC), 0)

    def conv_silu(sec):
        lo = sec * SEC
        acc = proj(4 * SEC + lo, SEC)
        prev = cbuf_ref[:, lo:lo + SEC]
        y = cw_ref[CONV_K - 1:CONV_K, lo:lo + SEC] * acc
        for d in range(1, CONV_K):
            sh = pltpu.roll(acc, d, 0)
            top = jnp.where(row8 < d, pltpu.roll(prev, d, 0), sh[0:8])
            sh = jnp.concatenate([top, sh[8:]], axis=0)
            y = y + cw_ref[CONV_K - 1 - d:CONV_K - d, lo:lo + SEC] * sh
        cbuf_ref[:, lo:lo + SEC] = acc[tm - 8:tm]
        return _silu(y)

    def l2norm_store(y, out_ref, scale):
        for g in range(N_GROUPS):
            a = y[:, g * LANES:(g + 1) * LANES]
            a = a * lax.rsqrt(jnp.sum(a * a, axis=-1, keepdims=True) + EPS)
            out_ref[:, g * LANES:(g + 1) * LANES] = (a * scale).astype(BF16)

    l2norm_store(conv_silu(0), bq_ref, LANES ** -0.5)
    l2norm_store(conv_silu(1), bk_ref, 1.0)
    bv_ref[...] = conv_silu(2).astype(BF16)

    bz_ref[...] = proj(7 * SEC, SEC).astype(BF16)

    sm = proj(8 * SEC, LANES)
    beta = jax.nn.sigmoid(sm)
    g = -jnp.exp(alog_ref[...]) * _softplus(sm + dtb_ref[...])
    lane = lax.broadcasted_iota(jnp.int32, (1, LANES), 1)
    bg_ref[...] = jnp.where(lane < GDN_HEADS, beta, g)


def _proj_ab(x2, pos2, invf, nw, w, cw, alog, dtb, *, seq):
    t, dm = x2.shape
    tm = TM_PROJ
    n_in = w.shape[1]
    row = lambda i: (i, 0)
    fix = lambda i: (0, 0)
    wide = lambda dt: jax.ShapeDtypeStruct((t, SEC), dt)
    out_shape = [wide(BF16)] * 8 + [jax.ShapeDtypeStruct((t, LANES), F32)]
    out_specs = [pl.BlockSpec((tm, SEC), row)] * 8 + [pl.BlockSpec((tm, LANES), row)]
    return pl.pallas_call(
        functools.partial(_proj_ab_kernel, tiles_per_seq=seq // tm),
        grid=(t // tm,),
        in_specs=[
            pl.BlockSpec((tm, dm), row),
            pl.BlockSpec((tm, 1), row),
            pl.BlockSpec((1, LANES), fix),
            pl.BlockSpec((1, dm), fix),
            pl.BlockSpec((dm, n_in), fix),
            pl.BlockSpec((CONV_K, 3 * SEC), fix),
            pl.BlockSpec((1, LANES), fix),
            pl.BlockSpec((1, LANES), fix),
        ],
        out_specs=out_specs,
        out_shape=out_shape,
        scratch_shapes=[pltpu.VMEM((8, 3 * SEC), F32)],
        compiler_params=pltpu.CompilerParams(
            dimension_semantics=("arbitrary",), vmem_limit_bytes=VMEM_LIMIT),
        name="proj_ab",
    )(x2, pos2, invf, nw, w, cw, alog, dtb)


def _proj_c_kernel(x_ref, nw_ref, w_ref, fb_ref, q_ref, k_ref, v_ref, z_ref, ct_ref, carry_ref,
                   *, tiles_per_seq):
    tm = x_ref.shape[0]
    i = pl.program_id(0)
    width = q_ref.shape[1]
    h = _rmsnorm_rows(x_ref[...], nw_ref[...]).astype(BF16)

    def proj(lo, n):
        return jnp.dot(h, w_ref[:, lo:lo + n], preferred_element_type=F32)

    for sec in range(width // SEC):
        lo = sec * SEC
        q_ref[:, lo:lo + SEC] = (proj(lo, SEC) * Q_SCALE).astype(BF16)
        k_ref[:, lo:lo + SEC] = proj(width + lo, SEC).astype(BF16)
        v_ref[:, lo:lo + SEC] = proj(2 * width + lo, SEC).astype(BF16)
        z_ref[:, lo:lo + SEC] = proj(3 * width + lo, SEC).astype(BF16)

    @pl.when(i % tiles_per_seq == 0)
    def _():
        carry_ref[...] = jnp.zeros_like(carry_ref)

    f = proj(4 * width, LANES) + fb_ref[...]
    log_f = -_softplus(-f)
    c = _prefix_sum_rows(log_f, tm) + carry_ref[0:1, :]
    carry_ref[...] = jnp.broadcast_to(c[tm - 1:tm, :], carry_ref.shape)
    ct_ref[0] = c.T[0:2 * FOX_PAIRS, :]


def _proj_c(x2, nw, w, fb, *, seq):
    t, dm = x2.shape
    tm = TM_PROJ
    width = 2 * FOX_PAIRS * SUB_DIM
    n_in = w.shape[1]
    tiles = seq // tm
    row = lambda i: (i, 0)
    fix = lambda i: (0, 0)
    wide = jax.ShapeDtypeStruct((t, width), BF16)
    return pl.pallas_call(
        functools.partial(_proj_c_kernel, tiles_per_seq=tiles),
        grid=(t // tm,),
        in_specs=[
            pl.BlockSpec((tm, dm), row),
            pl.BlockSpec((1, dm), fix),
            pl.BlockSpec((dm, n_in), fix),
            pl.BlockSpec((1, LANES), fix),
        ],
        out_specs=[pl.BlockSpec((tm, width), row)] * 4
        + [pl.BlockSpec((1, 2 * FOX_PAIRS, tm), lambda i: (i // tiles, 0, i % tiles))],
        out_shape=[wide] * 4
        + [jax.ShapeDtypeStruct((t // seq, 2 * FOX_PAIRS, seq), F32)],
        scratch_shapes=[pltpu.VMEM((8, LANES), F32)],
        compiler_params=pltpu.CompilerParams(
            dimension_semantics=("arbitrary",), vmem_limit_bytes=VMEM_LIMIT),
        name="proj_c",
    )(x2, nw, w, fb)


def _flash_two_softmax(q_ref, k_ref, v_ref, bias_fn, finish):
    t = TQ
    lo_lanes = lax.broadcasted_iota(jnp.int32, (1, LANES), 1) < SUB_DIM
    causal = (lax.broadcasted_iota(jnp.int32, (t, t), 1)
              <= lax.broadcasted_iota(jnp.int32, (t, t), 0))
    for i in range(q_ref.shape[1] // t):
        rows = slice(i * t, (i + 1) * t)
        q = q_ref[0, rows, :]
        zero = jnp.zeros_like(q)
        q2 = jnp.concatenate([jnp.where(lo_lanes, q, zero), jnp.where(lo_lanes, zero, q)], axis=0)
        m = jnp.full((2 * t, 1), -jnp.inf, F32)
        acc = jnp.zeros((2 * t, 2 * LANES), F32)
        for j in range(i + 1):
            keys = slice(j * t, (j + 1) * t)
            s = lax.dot_general(q2, k_ref[0, keys, :], (((1,), (1,)), ((), ())),
                                preferred_element_type=F32)
            halves = [s[:t], s[t:]]
            if bias_fn is not None:
                halves = [halves[c] + bias_fn(j, c) for c in range(2)]
            if j == i:
                halves = [jnp.where(causal, h, -jnp.inf) for h in halves]
            s = jnp.concatenate(halves, axis=0)
            m_new = jnp.maximum(m, jnp.max(s, axis=1, keepdims=True))
            alpha = jnp.exp2(m - m_new)
            p = jnp.exp2(s - m_new).astype(BF16)
            v_aug = jnp.concatenate([v_ref[0, keys, :], jnp.ones((t, LANES), BF16)], axis=1)
            acc = alpha * acc + jnp.dot(p, v_aug, preferred_element_type=F32)
            m = m_new
        finish(rows, lo_lanes, acc[:, LANES:LANES + 1], acc[:, :LANES])


def _diff_attn_kernel(q_ref, k_ref, v_ref, z_ref, lam_ref, subln_ref, o_ref):
    t = TQ
    lv = lam_ref[...]
    lam = (jnp.exp(jnp.sum(lv[0:1] * lv[1:2], axis=-1, keepdims=True))
           - jnp.exp(jnp.sum(lv[2:3] * lv[3:4], axis=-1, keepdims=True)) + LAMBDA_INIT_0)

    def finish(rows, lo_lanes, l, acc):
        o = acc[:t] / l[:t] - lam * (acc[t:] / l[t:])
        o = _rmsnorm_rows(o, subln_ref[...]) * (1.0 - LAMBDA_INIT_0)
        o_ref[0, rows, :] = (o * _silu(z_ref[0, rows, :].astype(F32))).astype(BF16)

    _flash_two_softmax(q_ref, k_ref, v_ref, None, finish)


def _fox_attn_kernel(q_ref, k_ref, v_ref, z_ref, ct_ref, o_ref):
    t = TQ
    hp = pl.program_id(1)

    def bias(j, c):
        return ct_ref[0, j, pl.ds(2 * hp + c, 1), :] * (-LOG2E)

    def finish(rows, lo_lanes, l, acc):
        o = jnp.where(lo_lanes, acc[:t] / l[:t], acc[t:] / l[t:])
        o_ref[0, rows, :] = (o * _silu(z_ref[0, rows, :].astype(F32))).astype(BF16)

    _flash_two_softmax(q_ref, k_ref, v_ref, bias, finish)


def _attention(kernel_fn, q, k, v, z, extras, extra_specs, name):
    b, s, width = q.shape
    spec = pl.BlockSpec((1, s, LANES), lambda bi, g: (bi, 0, g))
    return pl.pallas_call(
        kernel_fn,
        grid=(b, width // LANES),
        in_specs=[spec, spec, spec, spec] + extra_specs,
        out_specs=spec,
        out_shape=jax.ShapeDtypeStruct((b, s, width), BF16),
        compiler_params=pltpu.CompilerParams(
            dimension_semantics=("arbitrary", "arbitrary"), vmem_limit_bytes=VMEM_LIMIT),
        name=name,
    )(q, k, v, z, *extras)


def _gdn_kernel(q_ref, k_ref, v_ref, z_ref, bg_ref, hn_ref, o_ref,
                state_ref, gc_ref, low_ref, qk_ref, u_ref, w_ref, qg_ref, kdt_ref, gam_ref):
    s = q_ref.shape[1]
    n_chunks = s // CHUNK
    ri = lax.broadcasted_iota(jnp.int32, (CHUNK, CHUNK), 0)
    ci = lax.broadcasted_iota(jnp.int32, (CHUNK, CHUNK), 1)
    tril = ci <= ri
    strict = ci < ri

    def mm(a, b):
        return jnp.dot(a.astype(BF16), b.astype(BF16), preferred_element_type=F32)

    def chunk_rows(c):
        return pl.ds(pl.multiple_of(c * CHUNK, CHUNK), CHUNK)

    def head_lanes(h):
        return slice(h * LANES, (h + 1) * LANES)

    def phase_a(c, carry):
        rows = chunk_rows(c)
        bg = bg_ref[0, rows, :]
        gcum = _prefix_sum_rows(bg, CHUNK)
        gc_ref[rows, :] = gcum
        gcum_t = gcum.T
        prods = []
        for h in range(GDN_HEADS):
            q = q_ref[0, rows, head_lanes(h)]
            k = k_ref[0, rows, head_lanes(h)]
            k_beta = (k.astype(F32) * bg[:, h:h + 1]).astype(BF16)
            prods.append(lax.dot_general(
                jnp.concatenate([q, k_beta], axis=0), k,
                (((1,), (1,)), ((), ())), preferred_element_type=F32))
        for h in range(GDN_HEADS):
            n = c * GDN_HEADS + h
            gcol = gcum[:, GDN_HEADS + h:GDN_HEADS + h + 1]
            grow = gcum_t[GDN_HEADS + h:GDN_HEADS + h + 1, :]
            decay = jnp.exp(jnp.where(tril, gcol - grow, -jnp.inf))
            qk_ref[n] = jnp.where(tril, prods[h][:CHUNK] * decay, 0.0).astype(BF16)
            low = jnp.where(strict, prods[h][CHUNK:] * decay, 0.0)
            low_ref[n] = jnp.concatenate([low, jnp.zeros_like(low)], axis=1)
        return carry

    lax.fori_loop(0, n_chunks, phase_a, 0, unroll=GDN_UNROLL)

    rowl_ref = u_ref.at[0:CHUNK]
    tinv_ref = u_ref.at[CHUNK:2 * CHUNK]
    for i in range(CHUNK):
        rowl_ref[i] = low_ref[:, i, :].T[0:CHUNK, :]
    sub = lax.broadcasted_iota(jnp.int32, (8, LANES), 0)
    for i in range(CHUNK):
        groups = i // 8 + 1
        acc = [[None, None] for _ in range(groups)]
        for j in range(i):
            coef = rowl_ref[i, j:j + 1, :]
            for g in range(j // 8 + 1):
                term = coef * tinv_ref[j, 8 * g:8 * g + 8, :]
                slot = acc[g]
                slot[j % 2] = term if slot[j % 2] is None else slot[j % 2] + term
        for g in range(CHUNK // 8):
            if g < groups:
                val = jnp.where(sub == i % 8, 1.0, 0.0) if g == groups - 1 else jnp.zeros((8, LANES), F32)
                for part in acc[g]:
                    if part is not None:
                        val = val - part
            else:
                val = jnp.zeros((8, LANES), F32)
            tinv_ref[i, 8 * g:8 * g + 8, :] = val
    for i in range(CHUNK):
        t_i = jnp.concatenate([tinv_ref[i], jnp.zeros((LANES - CHUNK, LANES), F32)], axis=0)
        low_ref[:, i, :] = t_i.T

    def phase_c(c, carry):
        rows = chunk_rows(c)
        bg = bg_ref[0, rows, :]
        gcum = gc_ref[rows, :]
        heads = range(GDN_HEADS)
        gcols = [gcum[:, GDN_HEADS + h:GDN_HEADS + h + 1] for h in heads]
        e_gs = [jnp.exp(g) for g in gcols]
        kfs = [k_ref[0, rows, head_lanes(h)].astype(F32) for h in heads]
        xs = []
        for h in heads:
            beta = bg[:, h:h + 1]
            v_beta = v_ref[0, rows, head_lanes(h)].astype(F32) * beta
            rhs = jnp.concatenate([v_beta, kfs[h] * beta * e_gs[h]], axis=1)
            xs.append(mm(low_ref[c * GDN_HEADS + h][:, 0:CHUNK], rhs))
        for h in heads:
            n = c * GDN_HEADS + h
            glast = gcols[h][CHUNK - 1:CHUNK, :]
            u_ref[n] = xs[h][:, :LANES]
            w_ref[n] = xs[h][:, LANES:].astype(BF16)
            qg_ref[n] = (q_ref[0, rows, head_lanes(h)].astype(F32) * e_gs[h]).astype(BF16)
            kdt_ref[n] = (kfs[h] * jnp.exp(glast - gcols[h])).T.astype(BF16)
            gam_ref[n] = jnp.broadcast_to(jnp.exp(glast), (1, LANES))
        return carry

    lax.fori_loop(0, n_chunks, phase_c, 0, unroll=GDN_UNROLL)

    state_ref[...] = jnp.zeros_like(state_ref)

    def phase_d(c, carry):
        rows = chunk_rows(c)
        heads = range(GDN_HEADS)
        ns = [c * GDN_HEADS + h for h in heads]
        dot = functools.partial(jnp.dot, preferred_element_type=F32)
        sts = [state_ref[h] for h in heads]
        st_bs = [st.astype(BF16) for st in sts]
        w_st = [dot(w_ref[ns[h]], st_bs[h]) for h in heads]
        q_st = [dot(qg_ref[ns[h]], st_bs[h]) for h in heads]
        v_bs = [(u_ref[ns[h]] - w_st[h]).astype(BF16) for h in heads]
        k_v = [dot(kdt_ref[ns[h]], v_bs[h]) for h in heads]
        qk_v = [dot(qk_ref[ns[h]], v_bs[h]) for h in heads]
        for h in heads:
            state_ref[h] = sts[h] * gam_ref[ns[h]] + k_v[h]
        for h in heads:
            o = _rmsnorm_rows(q_st[h] + qk_v[h], hn_ref[...])
            gate = _silu(z_ref[0, rows, head_lanes(h)].astype(F32))
            o_ref[0, rows, head_lanes(h)] = (o * gate).astype(BF16)
        return carry

    lax.fori_loop(0, n_chunks, phase_d, 0, unroll=GDN_UNROLL)


def _gdn(q, k, v, z, bg, hn):
    b, s, width = q.shape
    n_sys = s // CHUNK * GDN_HEADS
    assert n_sys == LANES, "the lane-batched triangular inverse needs exactly 128 (chunk, head) systems"
    spec = pl.BlockSpec((1, s, width), lambda bi: (bi, 0, 0))
    return pl.pallas_call(
        _gdn_kernel,
        grid=(b,),
        in_specs=[spec, spec, spec, spec,
                  pl.BlockSpec((1, s, LANES), lambda bi: (bi, 0, 0)),
                  pl.BlockSpec((1, LANES), lambda bi: (0, 0))],
        out_specs=spec,
        out_shape=jax.ShapeDtypeStruct((b, s, width), BF16),
        scratch_shapes=[
            pltpu.VMEM((GDN_HEADS, LANES, LANES), F32),
            pltpu.VMEM((s, LANES), F32),
            pltpu.VMEM((n_sys, CHUNK, LANES), F32),
            pltpu.VMEM((n_sys, CHUNK, CHUNK), BF16),
            pltpu.VMEM((n_sys, CHUNK, LANES), F32),
            pltpu.VMEM((n_sys, CHUNK, LANES), BF16),
            pltpu.VMEM((n_sys, CHUNK, LANES), BF16),
            pltpu.VMEM((n_sys, LANES, CHUNK), BF16),
            pltpu.VMEM((n_sys, 1, LANES), F32),
        ],
        compiler_params=pltpu.CompilerParams(
            dimension_semantics=("arbitrary",), vmem_limit_bytes=GDN_VMEM_LIMIT),
        name="gdn",
    )(q, k, v, z, bg, hn)


def _out_proj_kernel(*refs, n_parts):
    x_ref = refs[0]
    parts = refs[1:1 + n_parts]
    w_ref, pw_ref, o_ref = refs[1 + n_parts:]
    y = None
    lo = 0
    for part in parts:
        n = part.shape[1]
        d = jnp.dot(part[...], w_ref[lo:lo + n, :], preferred_element_type=F32)
        y = d if y is None else y + d
        lo += n
    o_ref[...] = x_ref[...] + _rmsnorm_rows(y, pw_ref[...])


def _out_proj(x2, parts, w, pw, name):
    t, dm = x2.shape
    tm = TM_OUT
    row = lambda i: (i, 0)
    fix = lambda i: (0, 0)
    return pl.pallas_call(
        functools.partial(_out_proj_kernel, n_parts=len(parts)),
        grid=(t // tm,),
        in_specs=[pl.BlockSpec((tm, dm), row)]
        + [pl.BlockSpec((tm, p.shape[1]), row) for p in parts]
        + [pl.BlockSpec(w.shape, fix), pl.BlockSpec((1, dm), fix)],
        out_specs=pl.BlockSpec((tm, dm), row),
        out_shape=jax.ShapeDtypeStruct((t, dm), F32),
        compiler_params=pltpu.CompilerParams(
            dimension_semantics=("arbitrary",), vmem_limit_bytes=VMEM_LIMIT),
        name=name,
    )(x2, *parts, w, pw)


def _pad_lanes(vec, offset=0):
    out = jnp.zeros((1, LANES), F32)
    return out.at[0, offset:offset + vec.shape[0]].set(vec.astype(F32))


def kernel(x, positions, pre_norm, post_norm, w_in_ab, a_lambda_q1, a_lambda_k1, a_lambda_q2,
           a_lambda_k2, a_subln, b_conv_w, b_a_log, b_dt_bias, b_head_norm, w_out_ab,
           w_in_c, c_forget_bias, w_out_c):
    b, s, dm = x.shape
    t = b * s
    x2 = x.reshape(t, dm)

    main = 8 * SEC
    w_ab = jnp.concatenate(
        [w_in_ab[:, :main], jnp.pad(w_in_ab[:, main:], ((0, 0), (0, LANES - 2 * GDN_HEADS)))],
        axis=1).astype(BF16)
    inv_freq = ROPE_THETA ** (-(jnp.arange(ROT_HALF, dtype=F32) * 2.0) / ROT_DIMS)
    d = jnp.arange(LANES) % SUB_DIM
    invf = jnp.where(d < ROT_DIMS, inv_freq[d % ROT_HALF], 0.0).reshape(1, LANES)
    (a_q, a_k, a_v, a_z, b_q, b_k, b_v, b_z, b_bg) = _proj_ab(
        x2, positions.reshape(t, 1), invf, pre_norm[0:1], w_ab, b_conv_w.astype(F32),
        _pad_lanes(b_a_log, GDN_HEADS), _pad_lanes(b_dt_bias, GDN_HEADS), seq=s)

    lam_vecs = jnp.concatenate(
        [_pad_lanes(v) for v in (a_lambda_q1, a_lambda_k1, a_lambda_q2, a_lambda_k2)], axis=0)
    r3 = lambda a: a.reshape(b, s, a.shape[-1])
    o_a = _attention(
        _diff_attn_kernel, r3(a_q), r3(a_k), r3(a_v), r3(a_z),
        [lam_vecs, a_subln.astype(F32).reshape(1, LANES)],
        [pl.BlockSpec((4, LANES), lambda bi, g: (0, 0)),
         pl.BlockSpec((1, LANES), lambda bi, g: (0, 0))],
        "diff_attn")
    o_b = _gdn(r3(b_q), r3(b_k), r3(b_v), r3(b_z), r3(b_bg),
               b_head_norm.astype(F32).reshape(1, LANES))
    x2 = _out_proj(x2, [o_a.reshape(t, SEC), o_b.reshape(t, SEC)], w_out_ab.astype(BF16),
                   post_norm[0:1], "out_proj_ab")

    width = 2 * FOX_PAIRS * SUB_DIM
    w_c = jnp.concatenate(
        [w_in_c[:, :4 * width],
         jnp.pad(w_in_c[:, 4 * width:], ((0, 0), (0, LANES - 2 * FOX_PAIRS)))], axis=1).astype(BF16)
    c_q, c_k, c_v, c_z, c_t = _proj_c(x2, pre_norm[1:2], w_c, _pad_lanes(c_forget_bias), seq=s)
    c_t = c_t.reshape(b, 2 * FOX_PAIRS, s // TQ, TQ).transpose(0, 2, 1, 3)
    o_c = _attention(
        _fox_attn_kernel, r3(c_q), r3(c_k), r3(c_v), r3(c_z), [c_t],
        [pl.BlockSpec((1, s // TQ, 2 * FOX_PAIRS, TQ), lambda bi, g: (bi, 0, 0, 0))],
        "fox_attn")
    x2 = _out_proj(x2, [o_c.reshape(t, width)], w_out_c.astype(BF16), post_norm[1:2],
                   "out_proj_c")
    return x2.reshape(b, s, dm)
```

```python
import functools
import math

import jax
import jax.numpy as jnp
from jax import lax
from jax.experimental import pallas as pl
from jax.experimental.pallas import tpu as pltpu

F32 = jnp.float32
BF16 = jnp.bfloat16

EPS = 1e-6
ROPE_THETA = 500000.0
ROT_DIMS = 16
ROT_HALF = ROT_DIMS // 2
SUB_DIM = 64
LANES = 128
N_GROUPS = 4
SEC = N_GROUPS * LANES
CONV_K = 4
CHUNK = 64
GDN_HEADS = 4
GDN_UNROLL = 8
FOX_PAIRS = 8
LAMBDA_INIT_0 = 0.8 - 0.6 * math.exp(-0.3 * 0)
VMEM_LIMIT = 48 * 1024 * 1024
GDN_VMEM_LIMIT = 56 * 1024 * 1024

TM_PROJ = 256
TM_OUT = 1024
TQ = 512
LOG2E = math.log2(math.e)
Q_SCALE = SUB_DIM ** -0.5 * LOG2E


def _rmsnorm_rows(x, w):
    return x * lax.rsqrt(jnp.mean(x * x, axis=-1, keepdims=True) + EPS) * w


def _silu(x):
    return x * jax.nn.sigmoid(x)


def _softplus(x):
    return jnp.maximum(x, 0.0) + jnp.log(1.0 + jnp.exp(-jnp.abs(x)))


def _prefix_sum_rows(x, period):
    rows = x.shape[0]
    pos = lax.broadcasted_iota(jnp.int32, x.shape, 0) % period
    d = 1
    while d < min(period, rows):
        x = x + jnp.where(pos >= d, pltpu.roll(x, d, 0), 0.0)
        d *= 2
    return x


def _proj_ab_kernel(x_ref, pos_ref, invf_ref, nw_ref, w_ref, cw_ref, alog_ref, dtb_ref,
                    q_ref, k_ref, v_ref, az_ref, bq_ref, bk_ref, bv_ref, bz_ref, bg_ref,
                    cbuf_ref, *, tiles_per_seq):
    tm = x_ref.shape[0]
    i = pl.program_id(0)
    h = _rmsnorm_rows(x_ref[...], nw_ref[...]).astype(BF16)

    def proj(lo, width):
        return jnp.dot(h, w_ref[:, lo:lo + width], preferred_element_type=F32)

    ang = pos_ref[...].astype(F32) * invf_ref[...]
    cos = jnp.cos(ang)
    sin = jnp.sin(ang)
    d = lax.broadcasted_iota(jnp.int32, (1, LANES), 1) % SUB_DIM
    sin_hi = jnp.where((d >= ROT_HALF) & (d < ROT_DIMS), sin, 0.0)
    sin_lo = jnp.where(d < ROT_HALF, -sin, 0.0)

    def rope_store(acc, out_ref, scale):
        for g in range(N_GROUPS):
            a = acc[:, g * LANES:(g + 1) * LANES]
            r = (a * cos + pltpu.roll(a, ROT_HALF, 1) * sin_hi
                 + pltpu.roll(a, LANES - ROT_HALF, 1) * sin_lo)
            out_ref[:, g * LANES:(g + 1) * LANES] = (r * scale).astype(BF16)

    rope_store(proj(0, SEC), q_ref, Q_SCALE)
    rope_store(proj(SEC, SEC), k_ref, 1.0)
    v_ref[...] = proj(2 * SEC, SEC).astype(BF16)
    az_ref[...] = proj(3 * SEC, SEC).astype(BF16)

    @pl.when(i % tiles_per_seq == 0)
    def _():
        cbuf_ref[...] = jnp.zeros_like(cbuf_ref)

    row8 = lax.broadcasted_iota(jnp.int32, (8, SEC), 0)

    def conv_silu(sec):
        lo = sec * SEC
        acc = proj(4 * SEC + lo, SEC)
        prev = cbuf_ref[:, lo:lo + SEC]
        y = cw_ref[CONV_K - 1:CONV_K, lo:lo + SEC] * acc
        for d in range(1, CONV_K):
            sh = pltpu.roll(acc, d, 0)
            top = jnp.where(row8 < d, pltpu.roll(prev, d, 0), sh[0:8])
            sh = jnp.concatenate([top, sh[8:]], axis=0)
            y = y + cw_ref[CONV_K - 1 - d:CONV_K - d, lo:lo + SEC] * sh
        cbuf_ref[:, lo:lo + SEC] = acc[tm - 8:tm]
        return _silu(y)

    def l2norm_store(y, out_ref, scale):
        for g in range(N_GROUPS):
            a = y[:, g * LANES:(g + 1) * LANES]
            a = a * lax.rsqrt(jnp.sum(a * a, axis=-1, keepdims=True) + EPS)
            out_ref[:, g * LANES:(g + 1) * LANES] = (a * scale).astype(BF16)

    l2norm_store(conv_silu(0), bq_ref, LANES ** -0.5)
    l2norm_store(conv_silu(1), bk_ref, 1.0)
    bv_ref[...] = conv_silu(2).astype(BF16)

    bz_ref[...] = proj(7 * SEC, SEC).astype(BF16)

    sm = proj(8 * SEC, LANES)
    beta = jax.nn.sigmoid(sm)
    g = -jnp.exp(alog_ref[...]) * _softplus(sm + dtb_ref[...])
    lane = lax.broadcasted_iota(jnp.int32, (1, LANES), 1)
    bg_ref[...] = jnp.where(lane < GDN_HEADS, beta, g)


def _proj_ab(x2, pos2, invf, nw, w, cw, alog, dtb, *, seq):
    t, dm = x2.shape
    tm = TM_PROJ
    n_in = w.shape[1]
    row = lambda i: (i, 0)
    fix = lambda i: (0, 0)
    wide = lambda dt: jax.ShapeDtypeStruct((t, SEC), dt)
    out_shape = [wide(BF16)] * 8 + [jax.ShapeDtypeStruct((t, LANES), F32)]
    out_specs = [pl.BlockSpec((tm, SEC), row)] * 8 + [pl.BlockSpec((tm, LANES), row)]
    return pl.pallas_call(
        functools.partial(_proj_ab_kernel, tiles_per_seq=seq // tm),
        grid=(t // tm,),
        in_specs=[
            pl.BlockSpec((tm, dm), row),
            pl.BlockSpec((tm, 1), row),
            pl.BlockSpec((1, LANES), fix),
            pl.BlockSpec((1, dm), fix),
            pl.BlockSpec((dm, n_in), fix),
            pl.BlockSpec((CONV_K, 3 * SEC), fix),
            pl.BlockSpec((1, LANES), fix),
            pl.BlockSpec((1, LANES), fix),
        ],
        out_specs=out_specs,
        out_shape=out_shape,
        scratch_shapes=[pltpu.VMEM((8, 3 * SEC), F32)],
        compiler_params=pltpu.CompilerParams(
            dimension_semantics=("arbitrary",), vmem_limit_bytes=VMEM_LIMIT),
        name="proj_ab",
    )(x2, pos2, invf, nw, w, cw, alog, dtb)


def _proj_c_kernel(x_ref, nw_ref, w_ref, fb_ref, q_ref, k_ref, v_ref, z_ref, ct_ref, carry_ref,
                   *, tiles_per_seq):
    tm = x_ref.shape[0]
    i = pl.program_id(0)
    width = q_ref.shape[1]
    h = _rmsnorm_rows(x_ref[...], nw_ref[...]).astype(BF16)

    def proj(lo, n):
        return jnp.dot(h, w_ref[:, lo:lo + n], preferred_element_type=F32)

    for sec in range(width // SEC):
        lo = sec * SEC
        q_ref[:, lo:lo + SEC] = (proj(lo, SEC) * Q_SCALE).astype(BF16)
        k_ref[:, lo:lo + SEC] = proj(width + lo, SEC).astype(BF16)
        v_ref[:, lo:lo + SEC] = proj(2 * width + lo, SEC).astype(BF16)
        z_ref[:, lo:lo + SEC] = proj(3 * width + lo, SEC).astype(BF16)

    @pl.when(i % tiles_per_seq == 0)
    def _():
        carry_ref[...] = jnp.zeros_like(carry_ref)

    f = proj(4 * width, LANES) + fb_ref[...]
    log_f = -_softplus(-f)
    c = _prefix_sum_rows(log_f, tm) + carry_ref[0:1, :]
    carry_ref[...] = jnp.broadcast_to(c[tm - 1:tm, :], carry_ref.shape)
    ct_ref[0, 0] = c.T[0:2 * FOX_PAIRS, :]


def _proj_c(x2, nw, w, fb, *, seq):
    t, dm = x2.shape
    tm = TM_PROJ
    width = 2 * FOX_PAIRS * SUB_DIM
    n_in = w.shape[1]
    tiles = seq // tm
    per_key_tile = TQ // tm
    row = lambda i: (i, 0)
    fix = lambda i: (0, 0)
    wide = jax.ShapeDtypeStruct((t, width), BF16)
    return pl.pallas_call(
        functools.partial(_proj_c_kernel, tiles_per_seq=tiles),
        grid=(t // tm,),
        in_specs=[
            pl.BlockSpec((tm, dm), row),
            pl.BlockSpec((1, dm), fix),
            pl.BlockSpec((dm, n_in), fix),
            pl.BlockSpec((1, LANES), fix),
        ],
        out_specs=[pl.BlockSpec((tm, width), row)] * 4
        + [pl.BlockSpec((1, 1, 2 * FOX_PAIRS, tm),
                        lambda i: (i // tiles, (i % tiles) // per_key_tile, 0, i % per_key_tile))],
        out_shape=[wide] * 4
        + [jax.ShapeDtypeStruct((t // seq, seq // TQ, 2 * FOX_PAIRS, TQ), F32)],
        scratch_shapes=[pltpu.VMEM((8, LANES), F32)],
        compiler_params=pltpu.CompilerParams(
            dimension_semantics=("arbitrary",), vmem_limit_bytes=VMEM_LIMIT),
        name="proj_c",
    )(x2, nw, w, fb)


def _flash_two_softmax(q_ref, k_ref, v_ref, bias_fn, finish):
    t = TQ
    lo_lanes = lax.broadcasted_iota(jnp.int32, (1, LANES), 1) < SUB_DIM
    causal = (lax.broadcasted_iota(jnp.int32, (t, t), 1)
              <= lax.broadcasted_iota(jnp.int32, (t, t), 0))
    for i in range(q_ref.shape[1] // t):
        rows = slice(i * t, (i + 1) * t)
        q = q_ref[0, rows, :]
        zero = jnp.zeros_like(q)
        q2 = jnp.concatenate([jnp.where(lo_lanes, q, zero), jnp.where(lo_lanes, zero, q)], axis=0)
        m = jnp.full((2 * t, 1), -jnp.inf, F32)
        acc = jnp.zeros((2 * t, 2 * LANES), F32)
        for j in range(i + 1):
            keys = slice(j * t, (j + 1) * t)
            s = lax.dot_general(q2, k_ref[0, keys, :], (((1,), (1,)), ((), ())),
                                preferred_element_type=F32)
            halves = [s[:t], s[t:]]
            if bias_fn is not None:
                halves = [halves[c] + bias_fn(j, c) for c in range(2)]
            if j == i:
                halves = [jnp.where(causal, h, -jnp.inf) for h in halves]
            s = jnp.concatenate(halves, axis=0)
            m_new = jnp.maximum(m, jnp.max(s, axis=1, keepdims=True))
            alpha = jnp.exp2(m - m_new)
            p = jnp.exp2(s - m_new).astype(BF16)
            v_aug = jnp.concatenate([v_ref[0, keys, :], jnp.ones((t, LANES), BF16)], axis=1)
            acc = alpha * acc + jnp.dot(p, v_aug, preferred_element_type=F32)
            m = m_new
        finish(rows, lo_lanes, acc[:, LANES:LANES + 1], acc[:, :LANES])


def _diff_attn_kernel(q_ref, k_ref, v_ref, z_ref, lam_ref, subln_ref, o_ref):
    t = TQ
    lv = lam_ref[...]
    lam = (jnp.exp(jnp.sum(lv[0:1] * lv[1:2], axis=-1, keepdims=True))
           - jnp.exp(jnp.sum(lv[2:3] * lv[3:4], axis=-1, keepdims=True)) + LAMBDA_INIT_0)

    def finish(rows, lo_lanes, l, acc):
        o = acc[:t] / l[:t] - lam * (acc[t:] / l[t:])
        o = _rmsnorm_rows(o, subln_ref[...]) * (1.0 - LAMBDA_INIT_0)
        o_ref[0, rows, :] = (o * _silu(z_ref[0, rows, :].astype(F32))).astype(BF16)

    _flash_two_softmax(q_ref, k_ref, v_ref, None, finish)


def _fox_attn_kernel(q_ref, k_ref, v_ref, z_ref, ct_ref, o_ref):
    t = TQ
    hp = pl.program_id(1)

    def bias(j, c):
        return ct_ref[0, j, pl.ds(2 * hp + c, 1), :] * (-LOG2E)

    def finish(rows, lo_lanes, l, acc):
        o = jnp.where(lo_lanes, acc[:t] / l[:t], acc[t:] / l[t:])
        o_ref[0, rows, :] = (o * _silu(z_ref[0, rows, :].astype(F32))).astype(BF16)

    _flash_two_softmax(q_ref, k_ref, v_ref, bias, finish)


def _attention(kernel_fn, q, k, v, z, extras, extra_specs, name):
    b, s, width = q.shape
    spec = pl.BlockSpec((1, s, LANES), lambda bi, g: (bi, 0, g))
    return pl.pallas_call(
        kernel_fn,
        grid=(b, width // LANES),
        in_specs=[spec, spec, spec, spec] + extra_specs,
        out_specs=spec,
        out_shape=jax.ShapeDtypeStruct((b, s, width), BF16),
        compiler_params=pltpu.CompilerParams(
            dimension_semantics=("arbitrary", "arbitrary"), vmem_limit_bytes=VMEM_LIMIT),
        name=name,
    )(q, k, v, z, *extras)


def _gdn_kernel(q_ref, k_ref, v_ref, z_ref, bg_ref, hn_ref, o_ref,
                state_ref, gc_ref, low_ref, qk_ref, u_ref, w_ref, qg_ref, kdt_ref, gam_ref):
    s = q_ref.shape[1]
    n_chunks = s // CHUNK
    ri = lax.broadcasted_iota(jnp.int32, (CHUNK, CHUNK), 0)
    ci = lax.broadcasted_iota(jnp.int32, (CHUNK, CHUNK), 1)
    tril = ci <= ri
    strict = ci < ri

    def mm(a, b):
        return jnp.dot(a.astype(BF16), b.astype(BF16), preferred_element_type=F32)

    def chunk_rows(c):
        return pl.ds(pl.multiple_of(c * CHUNK, CHUNK), CHUNK)

    def head_lanes(h):
        return slice(h * LANES, (h + 1) * LANES)

    def phase_a(c, carry):
        rows = chunk_rows(c)
        bg = bg_ref[0, rows, :]
        gcum = _prefix_sum_rows(bg, CHUNK)
        gc_ref[rows, :] = gcum
        gcum_t = gcum.T
        prods = []
        for h in range(GDN_HEADS):
            q = q_ref[0, rows, head_lanes(h)]
            k = k_ref[0, rows, head_lanes(h)]
            k_beta = (k.astype(F32) * bg[:, h:h + 1]).astype(BF16)
            prods.append(lax.dot_general(
                jnp.concatenate([q, k_beta], axis=0), k,
                (((1,), (1,)), ((), ())), preferred_element_type=F32))
        for h in range(GDN_HEADS):
            n = c * GDN_HEADS + h
            gcol = gcum[:, GDN_HEADS + h:GDN_HEADS + h + 1]
            grow = gcum_t[GDN_HEADS + h:GDN_HEADS + h + 1, :]
            decay = jnp.exp(jnp.where(tril, gcol - grow, -jnp.inf))
            qk_ref[n] = jnp.where(tril, prods[h][:CHUNK] * decay, 0.0).astype(BF16)
            low = jnp.where(strict, prods[h][CHUNK:] * decay, 0.0)
            low_ref[n] = jnp.concatenate([low, jnp.zeros_like(low)], axis=1)
        return carry

    lax.fori_loop(0, n_chunks, phase_a, 0, unroll=GDN_UNROLL)

    rowl_ref = u_ref.at[0:CHUNK]
    tinv_ref = u_ref.at[CHUNK:2 * CHUNK]
    by_row = pltpu.einshape("nij->inj", low_ref[...])
    for i in range(CHUNK):
        rowl_ref[i] = by_row[i].T[0:CHUNK, :]
    sub = lax.broadcasted_iota(jnp.int32, (8, LANES), 0)
    for i in range(CHUNK):
        groups = i // 8 + 1
        acc = [[None, None] for _ in range(groups)]
        for j in range(i):
            coef = rowl_ref[i, j:j + 1, :]
            for g in range(j // 8 + 1):
                term = coef * tinv_ref[j, 8 * g:8 * g + 8, :]
                slot = acc[g]
                slot[j % 2] = term if slot[j % 2] is None else slot[j % 2] + term
        for g in range(CHUNK // 8):
            if g < groups:
                val = jnp.where(sub == i % 8, 1.0, 0.0) if g == groups - 1 else jnp.zeros((8, LANES), F32)
                for part in acc[g]:
                    if part is not None:
                        val = val - part
            else:
                val = jnp.zeros((8, LANES), F32)
            tinv_ref[i, 8 * g:8 * g + 8, :] = val
    pad = jnp.zeros((LANES - CHUNK, LANES), F32)
    t_rows = jnp.stack([jnp.concatenate([tinv_ref[i], pad], axis=0).T for i in range(CHUNK)], axis=0)
    low_ref[...] = pltpu.einshape("inj->nij", t_rows)

    def phase_c(c, carry):
        rows = chunk_rows(c)
        bg = bg_ref[0, rows, :]
        gcum = gc_ref[rows, :]
        heads = range(GDN_HEADS)
        gcols = [gcum[:, GDN_HEADS + h:GDN_HEADS + h + 1] for h in heads]
        e_gs = [jnp.exp(g) for g in gcols]
        kfs = [k_ref[0, rows, head_lanes(h)].astype(F32) for h in heads]
        xs = []
        for h in heads:
            beta = bg[:, h:h + 1]
            v_beta = v_ref[0, rows, head_lanes(h)].astype(F32) * beta
            rhs = jnp.concatenate([v_beta, kfs[h] * beta * e_gs[h]], axis=1)
            xs.append(mm(low_ref[c * GDN_HEADS + h][:, 0:CHUNK], rhs))
        for h in heads:
            n = c * GDN_HEADS + h
            glast = gcols[h][CHUNK - 1:CHUNK, :]
            u_ref[n] = xs[h][:, :LANES]
            w_ref[n] = xs[h][:, LANES:].astype(BF16)
            qg_ref[n] = (q_ref[0, rows, head_lanes(h)].astype(F32) * e_gs[h]).astype(BF16)
            kdt_ref[n] = (kfs[h] * jnp.exp(glast - gcols[h])).T.astype(BF16)
            gam_ref[n] = jnp.broadcast_to(jnp.exp(glast), (1, LANES))
        return carry

    lax.fori_loop(0, n_chunks, phase_c, 0, unroll=GDN_UNROLL)

    state_ref[...] = jnp.zeros_like(state_ref)

    def phase_d(c, carry):
        rows = chunk_rows(c)
        heads = range(GDN_HEADS)
        ns = [c * GDN_HEADS + h for h in heads]
        dot = functools.partial(jnp.dot, preferred_element_type=F32)
        sts = [state_ref[h] for h in heads]
        st_bs = [st.astype(BF16) for st in sts]
        w_st = [dot(w_ref[ns[h]], st_bs[h]) for h in heads]
        q_st = [dot(qg_ref[ns[h]], st_bs[h]) for h in heads]
        v_bs = [(u_ref[ns[h]] - w_st[h]).astype(BF16) for h in heads]
        k_v = [dot(kdt_ref[ns[h]], v_bs[h]) for h in heads]
        qk_v = [dot(qk_ref[ns[h]], v_bs[h]) for h in heads]
        for h in heads:
            state_ref[h] = sts[h] * gam_ref[ns[h]] + k_v[h]
        for h in heads:
            o = _rmsnorm_rows(q_st[h] + qk_v[h], hn_ref[...])
            gate = _silu(z_ref[0, rows, head_lanes(h)].astype(F32))
            o_ref[0, rows, head_lanes(h)] = (o * gate).astype(BF16)
        return carry

    lax.fori_loop(0, n_chunks, phase_d, 0, unroll=GDN_UNROLL)


def _gdn(q, k, v, z, bg, hn):
    b, s, width = q.shape
    n_sys = s // CHUNK * GDN_HEADS
    assert n_sys == LANES, "the lane-batched triangular inverse needs exactly 128 (chunk, head) systems"
    spec = pl.BlockSpec((1, s, width), lambda bi: (bi, 0, 0))
    return pl.pallas_call(
        _gdn_kernel,
        grid=(b,),
        in_specs=[spec, spec, spec, spec,
                  pl.BlockSpec((1, s, LANES), lambda bi: (bi, 0, 0)),
                  pl.BlockSpec((1, LANES), lambda bi: (0, 0))],
        out_specs=spec,
        out_shape=jax.ShapeDtypeStruct((b, s, width), BF16),
        scratch_shapes=[
            pltpu.VMEM((GDN_HEADS, LANES, LANES), F32),
            pltpu.VMEM((s, LANES), F32),
            pltpu.VMEM((n_sys, CHUNK, LANES), F32),
            pltpu.VMEM((n_sys, CHUNK, CHUNK), BF16),
            pltpu.VMEM((n_sys, CHUNK, LANES), F32),
            pltpu.VMEM((n_sys, CHUNK, LANES), BF16),
            pltpu.VMEM((n_sys, CHUNK, LANES), BF16),
            pltpu.VMEM((n_sys, LANES, CHUNK), BF16),
            pltpu.VMEM((n_sys, 1, LANES), F32),
        ],
        compiler_params=pltpu.CompilerParams(
            dimension_semantics=("arbitrary",), vmem_limit_bytes=GDN_VMEM_LIMIT),
        name="gdn",
    )(q, k, v, z, bg, hn)


def _out_proj_kernel(*refs, n_parts):
    x_ref = refs[0]
    parts = refs[1:1 + n_parts]
    w_ref, pw_ref, o_ref = refs[1 + n_parts:]
    y = None
    lo = 0
    for part in parts:
        n = part.shape[1]
        d = jnp.dot(part[...], w_ref[lo:lo + n, :], preferred_element_type=F32)
        y = d if y is None else y + d
        lo += n
    o_ref[...] = x_ref[...] + _rmsnorm_rows(y, pw_ref[...])


def _out_proj(x2, parts, w, pw, name):
    t, dm = x2.shape
    tm = TM_OUT
    row = lambda i: (i, 0)
    fix = lambda i: (0, 0)
    return pl.pallas_call(
        functools.partial(_out_proj_kernel, n_parts=len(parts)),
        grid=(t // tm,),
        in_specs=[pl.BlockSpec((tm, dm), row)]
        + [pl.BlockSpec((tm, p.shape[1]), row) for p in parts]
        + [pl.BlockSpec(w.shape, fix), pl.BlockSpec((1, dm), fix)],
        out_specs=pl.BlockSpec((tm, dm), row),
        out_shape=jax.ShapeDtypeStruct((t, dm), F32),
        compiler_params=pltpu.CompilerParams(
            dimension_semantics=("arbitrary",), vmem_limit_bytes=VMEM_LIMIT),
        name=name,
    )(x2, *parts, w, pw)


def _pad_lanes(vec, offset=0):
    out = jnp.zeros((1, LANES), F32)
    return out.at[0, offset:offset + vec.shape[0]].set(vec.astype(F32))


def kernel(x, positions, pre_norm, post_norm, w_in_ab, a_lambda_q1, a_lambda_k1, a_lambda_q2,
           a_lambda_k2, a_subln, b_conv_w, b_a_log, b_dt_bias, b_head_norm, w_out_ab,
           w_in_c, c_forget_bias, w_out_c):
    b, s, dm = x.shape
    t = b * s
    x2 = x.reshape(t, dm)

    main = 8 * SEC
    w_ab = jnp.concatenate(
        [w_in_ab[:, :main], jnp.pad(w_in_ab[:, main:], ((0, 0), (0, LANES - 2 * GDN_HEADS)))],
        axis=1).astype(BF16)
    inv_freq = ROPE_THETA ** (-(jnp.arange(ROT_HALF, dtype=F32) * 2.0) / ROT_DIMS)
    d = jnp.arange(LANES) % SUB_DIM
    invf = jnp.where(d < ROT_DIMS, inv_freq[d % ROT_HALF], 0.0).reshape(1, LANES)
    (a_q, a_k, a_v, a_z, b_q, b_k, b_v, b_z, b_bg) = _proj_ab(
        x2, positions.reshape(t, 1), invf, pre_norm[0:1], w_ab, b_conv_w.astype(F32),
        _pad_lanes(b_a_log, GDN_HEADS), _pad_lanes(b_dt_bias, GDN_HEADS), seq=s)

    lam_vecs = jnp.concatenate(
        [_pad_lanes(v) for v in (a_lambda_q1, a_lambda_k1, a_lambda_q2, a_lambda_k2)], axis=0)
    r3 = lambda a: a.reshape(b, s, a.shape[-1])
    o_a = _attention(
        _diff_attn_kernel, r3(a_q), r3(a_k), r3(a_v), r3(a_z),
        [lam_vecs, a_subln.astype(F32).reshape(1, LANES)],
        [pl.BlockSpec((4, LANES), lambda bi, g: (0, 0)),
         pl.BlockSpec((1, LANES), lambda bi, g: (0, 0))],
        "diff_attn")
    o_b = _gdn(r3(b_q), r3(b_k), r3(b_v), r3(b_z), r3(b_bg),
               b_head_norm.astype(F32).reshape(1, LANES))
    x2 = _out_proj(x2, [o_a.reshape(t, SEC), o_b.reshape(t, SEC)], w_out_ab.astype(BF16),
                   post_norm[0:1], "out_proj_ab")

    width = 2 * FOX_PAIRS * SUB_DIM
    w_c = jnp.concatenate(
        [w_in_c[:, :4 * width],
         jnp.pad(w_in_c[:, 4 * width:], ((0, 0), (0, LANES - 2 * FOX_PAIRS)))], axis=1).astype(BF16)
    c_q, c_k, c_v, c_z, c_t = _proj_c(x2, pre_norm[1:2], w_c, _pad_lanes(c_forget_bias), seq=s)
    o_c = _attention(
        _fox_attn_kernel, r3(c_q), r3(c_k), r3(c_v), r3(c_z), [c_t],
        [pl.BlockSpec((1, s // TQ, 2 * FOX_PAIRS, TQ), lambda bi, g: (bi, 0, 0, 0))],
        "fox_attn")
    x2 = _out_proj(x2, [o_c.reshape(t, width)], w_out_c.astype(BF16), post_norm[1:2],
                   "out_proj_c")
    return x2.reshape(b, s, dm)
```

```python
import functools
import math

import jax
import jax.numpy as jnp
from jax import lax
from jax.experimental import pallas as pl
from jax.experimental.pallas import tpu as pltpu

F32 = jnp.float32
BF16 = jnp.bfloat16

EPS = 1e-6
ROPE_THETA = 500000.0
ROT_DIMS = 16
ROT_HALF = ROT_DIMS // 2
SUB_DIM = 64
LANES = 128
N_GROUPS = 4
SEC = N_GROUPS * LANES
CONV_K = 4
CHUNK = 64
GDN_HEADS = 4
GDN_UNROLL = 8
FOX_PAIRS = 8
LAMBDA_INIT_0 = 0.8 - 0.6 * math.exp(-0.3 * 0)
VMEM_LIMIT = 48 * 1024 * 1024
GDN_VMEM_LIMIT = 56 * 1024 * 1024

TM_PROJ = 256
TM_OUT = 1024
TQ = 512
LOG2E = math.log2(math.e)
Q_SCALE = SUB_DIM ** -0.5 * LOG2E


def _rmsnorm_rows(x, w):
    return x * lax.rsqrt(jnp.mean(x * x, axis=-1, keepdims=True) + EPS) * w


def _silu(x):
    return x * jax.nn.sigmoid(x)


def _softplus(x):
    return jnp.maximum(x, 0.0) + jnp.log(1.0 + jnp.exp(-jnp.abs(x)))


def _prefix_sum_rows(x, period):
    rows = x.shape[0]
    pos = lax.broadcasted_iota(jnp.int32, x.shape, 0) % period
    d = 1
    while d < min(period, rows):
        x = x + jnp.where(pos >= d, pltpu.roll(x, d, 0), 0.0)
        d *= 2
    return x


def _proj_ab_kernel(x_ref, pos_ref, invf_ref, nw_ref, w_ref, cw_ref, alog_ref, dtb_ref,
                    q_ref, k_ref, v_ref, az_ref, bq_ref, bk_ref, bv_ref, bz_ref, bg_ref,
                    cbuf_ref, *, tiles_per_seq):
    tm = x_ref.shape[0]
    i = pl.program_id(0)
    h = _rmsnorm_rows(x_ref[...], nw_ref[...]).astype(BF16)

    def proj(lo, width):
        return jnp.dot(h, w_ref[:, lo:lo + width], preferred_element_type=F32)

    ang = pos_ref[...].astype(F32) * invf_ref[...]
    cos = jnp.cos(ang)
    sin = jnp.sin(ang)
    d = lax.broadcasted_iota(jnp.int32, (1, LANES), 1) % SUB_DIM
    sin_hi = jnp.where((d >= ROT_HALF) & (d < ROT_DIMS), sin, 0.0)
    sin_lo = jnp.where(d < ROT_HALF, -sin, 0.0)

    def rope_store(acc, out_ref, scale):
        for g in range(N_GROUPS):
            a = acc[:, g * LANES:(g + 1) * LANES]
            r = (a * cos + pltpu.roll(a, ROT_HALF, 1) * sin_hi
                 + pltpu.roll(a, LANES - ROT_HALF, 1) * sin_lo)
            out_ref[:, g * LANES:(g + 1) * LANES] = (r * scale).astype(BF16)

    rope_store(proj(0, SEC), q_ref, Q_SCALE)
    rope_store(proj(SEC, SEC), k_ref, 1.0)
    v_ref[...] = proj(2 * SEC, SEC).astype(BF16)
    az_ref[...] = proj(3 * SEC, SEC).astype(BF16)

    @pl.when(i % tiles_per_seq == 0)
    def _():
        cbuf_ref[...] = jnp.zeros_like(cbuf_ref)

    row8 = lax.broadcasted_iota(jnp.int32, (8, SEC), 0)

    def conv_silu(sec):
        lo = sec * SEC
        acc = proj(4 * SEC + lo, SEC)
        prev = cbuf_ref[:, lo:lo + SEC]
        y = cw_ref[CONV_K - 1:CONV_K, lo:lo + SEC] * acc
        for d in range(1, CONV_K):
            sh = pltpu.roll(acc, d, 0)
            top = jnp.where(row8 < d, pltpu.roll(prev, d, 0), sh[0:8])
            sh = jnp.concatenate([top, sh[8:]], axis=0)
            y = y + cw_ref[CONV_K - 1 - d:CONV_K - d, lo:lo + SEC] * sh
        cbuf_ref[:, lo:lo + SEC] = acc[tm - 8:tm]
        return _silu(y)

    def l2norm_store(y, out_ref, scale):
        for g in range(N_GROUPS):
            a = y[:, g * LANES:(g + 1) * LANES]
            a = a * lax.rsqrt(jnp.sum(a * a, axis=-1, keepdims=True) + EPS)
            out_ref[:, g * LANES:(g + 1) * LANES] = (a * scale).astype(BF16)

    l2norm_store(conv_silu(0), bq_ref, LANES ** -0.5)
    l2norm_store(conv_silu(1), bk_ref, 1.0)
    bv_ref[...] = conv_silu(2).astype(BF16)

    bz_ref[...] = proj(7 * SEC, SEC).astype(BF16)

    sm = proj(8 * SEC, LANES)
    beta = jax.nn.sigmoid(sm)
    g = -jnp.exp(alog_ref[...]) * _softplus(sm + dtb_ref[...])
    lane = lax.broadcasted_iota(jnp.int32, (1, LANES), 1)
    bg_ref[...] = jnp.where(lane < GDN_HEADS, beta, g)


def _proj_ab(x2, pos2, invf, nw, w, cw, alog, dtb, *, seq):
    t, dm = x2.shape
    tm = TM_PROJ
    n_in = w.shape[1]
    row = lambda i: (i, 0)
    fix = lambda i: (0, 0)
    wide = lambda dt: jax.ShapeDtypeStruct((t, SEC), dt)
    out_shape = [wide(BF16)] * 8 + [jax.ShapeDtypeStruct((t, LANES), F32)]
    out_specs = [pl.BlockSpec((tm, SEC), row)] * 8 + [pl.BlockSpec((tm, LANES), row)]
    return pl.pallas_call(
        functools.partial(_proj_ab_kernel, tiles_per_seq=seq // tm),
        grid=(t // tm,),
        in_specs=[
            pl.BlockSpec((tm, dm), row),
            pl.BlockSpec((tm, 1), row),
            pl.BlockSpec((1, LANES), fix),
            pl.BlockSpec((1, dm), fix),
            pl.BlockSpec((dm, n_in), fix),
            pl.BlockSpec((CONV_K, 3 * SEC), fix),
            pl.BlockSpec((1, LANES), fix),
            pl.BlockSpec((1, LANES), fix),
        ],
        out_specs=out_specs,
        out_shape=out_shape,
        scratch_shapes=[pltpu.VMEM((8, 3 * SEC), F32)],
        compiler_params=pltpu.CompilerParams(
            dimension_semantics=("arbitrary",), vmem_limit_bytes=VMEM_LIMIT),
        name="proj_ab",
    )(x2, pos2, invf, nw, w, cw, alog, dtb)


def _out_ab_proj_c_kernel(x_ref, oa_ref, ob_ref, wo_ref, pw_ref, nw_ref, w_ref, fb_ref,
                          x1_ref, q_ref, k_ref, v_ref, z_ref, ct_ref, h_even, h_odd, carry_ref,
                          *, tiles_per_seq):
    tm = x_ref.shape[0]
    s = pl.program_id(0)
    width = q_ref.shape[1]
    n_a = oa_ref.shape[1]

    @pl.when(s == 0)
    def _():
        h_odd[...] = jnp.zeros_like(h_odd)
        carry_ref[...] = jnp.zeros_like(carry_ref)

    def step(h_next_ref, h_ref):
        y = (jnp.dot(oa_ref[...], wo_ref[0:n_a, :], preferred_element_type=F32)
             + jnp.dot(ob_ref[...], wo_ref[n_a:, :], preferred_element_type=F32))
        x1 = x_ref[...] + _rmsnorm_rows(y, pw_ref[...])
        x1_ref[...] = x1
        h_next_ref[...] = _rmsnorm_rows(x1, nw_ref[...]).astype(BF16)
        h = h_ref[...]

        def proj(lo, n):
            return jnp.dot(h, w_ref[:, lo:lo + n], preferred_element_type=F32)

        for sec in range(width // SEC):
            lo = sec * SEC
            q_ref[:, lo:lo + SEC] = (proj(lo, SEC) * Q_SCALE).astype(BF16)
            k_ref[:, lo:lo + SEC] = proj(width + lo, SEC).astype(BF16)
            v_ref[:, lo:lo + SEC] = proj(2 * width + lo, SEC).astype(BF16)
            z_ref[:, lo:lo + SEC] = proj(3 * width + lo, SEC).astype(BF16)

        @pl.when((s + tiles_per_seq - 1) % tiles_per_seq == 0)
        def _():
            carry_ref[...] = jnp.zeros_like(carry_ref)

        f = proj(4 * width, LANES) + fb_ref[...]
        log_f = -_softplus(-f)
        c = _prefix_sum_rows(log_f, tm) + carry_ref[0:1, :]
        carry_ref[...] = jnp.broadcast_to(c[tm - 1:tm, :], carry_ref.shape)
        ct_ref[0, 0] = c.T[0:2 * FOX_PAIRS, :]

    @pl.when(s % 2 == 0)
    def _():
        step(h_even, h_odd)

    @pl.when(s % 2 == 1)
    def _():
        step(h_odd, h_even)


def _out_ab_proj_c(x2, o_a, o_b, w_out, pw, nw, w, fb, *, seq):
    t, dm = x2.shape
    tm = TM_PROJ
    width = 2 * FOX_PAIRS * SUB_DIM
    n_in = w.shape[1]
    n_tiles = t // tm
    tiles = seq // tm
    per_key_tile = TQ // tm
    cur = lambda i: (jnp.minimum(i, n_tiles - 1), 0)
    prev = lambda i: (jnp.maximum(i - 1, 0), 0)
    fix = lambda i: (0, 0)

    def gate_rows(i):
        j = jnp.maximum(i - 1, 0)
        return (j // tiles, (j % tiles) // per_key_tile, 0, j % per_key_tile)

    wide = jax.ShapeDtypeStruct((t, width), BF16)
    return pl.pallas_call(
        functools.partial(_out_ab_proj_c_kernel, tiles_per_seq=tiles),
        grid=(n_tiles + 1,),
        in_specs=[
            pl.BlockSpec((tm, dm), cur),
            pl.BlockSpec((tm, o_a.shape[1]), cur),
            pl.BlockSpec((tm, o_b.shape[1]), cur),
            pl.BlockSpec(w_out.shape, fix),
            pl.BlockSpec((1, dm), fix),
            pl.BlockSpec((1, dm), fix),
            pl.BlockSpec((dm, n_in), fix),
            pl.BlockSpec((1, LANES), fix),
        ],
        out_specs=[pl.BlockSpec((tm, dm), cur)] + [pl.BlockSpec((tm, width), prev)] * 4
        + [pl.BlockSpec((1, 1, 2 * FOX_PAIRS, tm), gate_rows)],
        out_shape=[jax.ShapeDtypeStruct((t, dm), F32)] + [wide] * 4
        + [jax.ShapeDtypeStruct((t // seq, seq // TQ, 2 * FOX_PAIRS, TQ), F32)],
        scratch_shapes=[pltpu.VMEM((tm, dm), BF16), pltpu.VMEM((tm, dm), BF16),
                        pltpu.VMEM((8, LANES), F32)],
        compiler_params=pltpu.CompilerParams(
            dimension_semantics=("arbitrary",), vmem_limit_bytes=VMEM_LIMIT),
        name="out_ab_proj_c",
    )(x2, o_a, o_b, w_out, pw, nw, w, fb)


def _flash_two_softmax(q_ref, k_ref, v_ref, bias_fn, finish):
    t = TQ
    lo_lanes = lax.broadcasted_iota(jnp.int32, (1, LANES), 1) < SUB_DIM
    causal = (lax.broadcasted_iota(jnp.int32, (t, t), 1)
              <= lax.broadcasted_iota(jnp.int32, (t, t), 0))
    for i in range(q_ref.shape[1] // t):
        rows = slice(i * t, (i + 1) * t)
        q = q_ref[0, rows, :]
        zero = jnp.zeros_like(q)
        q2 = jnp.concatenate([jnp.where(lo_lanes, q, zero), jnp.where(lo_lanes, zero, q)], axis=0)
        m = jnp.full((2 * t, 1), -jnp.inf, F32)
        acc = jnp.zeros((2 * t, 2 * LANES), F32)
        for j in range(i + 1):
            keys = slice(j * t, (j + 1) * t)
            s = lax.dot_general(q2, k_ref[0, keys, :], (((1,), (1,)), ((), ())),
                                preferred_element_type=F32)
            halves = [s[:t], s[t:]]
            if bias_fn is not None:
                halves = [halves[c] + bias_fn(j, c) for c in range(2)]
            if j == i:
                halves = [jnp.where(causal, h, -jnp.inf) for h in halves]
            s = jnp.concatenate(halves, axis=0)
            m_new = jnp.maximum(m, jnp.max(s, axis=1, keepdims=True))
            alpha = jnp.exp2(m - m_new)
            p = jnp.exp2(s - m_new).astype(BF16)
            v_aug = jnp.concatenate([v_ref[0, keys, :], jnp.ones((t, LANES), BF16)], axis=1)
            acc = alpha * acc + jnp.dot(p, v_aug, preferred_element_type=F32)
            m = m_new
        finish(rows, lo_lanes, acc[:, LANES:LANES + 1], acc[:, :LANES])


def _diff_attn_kernel(q_ref, k_ref, v_ref, z_ref, lam_ref, subln_ref, o_ref):
    t = TQ
    lv = lam_ref[...]
    lam = (jnp.exp(jnp.sum(lv[0:1] * lv[1:2], axis=-1, keepdims=True))
           - jnp.exp(jnp.sum(lv[2:3] * lv[3:4], axis=-1, keepdims=True)) + LAMBDA_INIT_0)

    def finish(rows, lo_lanes, l, acc):
        o = acc[:t] / l[:t] - lam * (acc[t:] / l[t:])
        o = _rmsnorm_rows(o, subln_ref[...]) * (1.0 - LAMBDA_INIT_0)
        o_ref[0, rows, :] = (o * _silu(z_ref[0, rows, :].astype(F32))).astype(BF16)

    _flash_two_softmax(q_ref, k_ref, v_ref, None, finish)


def _fox_attn_kernel(q_ref, k_ref, v_ref, z_ref, ct_ref, o_ref):
    t = TQ
    hp = pl.program_id(1)

    def bias(j, c):
        return ct_ref[0, j, pl.ds(2 * hp + c, 1), :] * (-LOG2E)

    def finish(rows, lo_lanes, l, acc):
        o = jnp.where(lo_lanes, acc[:t] / l[:t], acc[t:] / l[t:])
        o_ref[0, rows, :] = (o * _silu(z_ref[0, rows, :].astype(F32))).astype(BF16)

    _flash_two_softmax(q_ref, k_ref, v_ref, bias, finish)


def _attention(kernel_fn, q, k, v, z, extras, extra_specs, name):
    b, s, width = q.shape
    spec = pl.BlockSpec((1, s, LANES), lambda bi, g: (bi, 0, g))
    return pl.pallas_call(
        kernel_fn,
        grid=(b, width // LANES),
        in_specs=[spec, spec, spec, spec] + extra_specs,
        out_specs=spec,
        out_shape=jax.ShapeDtypeStruct((b, s, width), BF16),
        compiler_params=pltpu.CompilerParams(
            dimension_semantics=("arbitrary", "arbitrary"), vmem_limit_bytes=VMEM_LIMIT),
        name=name,
    )(q, k, v, z, *extras)


def _gdn_kernel(q_ref, k_ref, v_ref, z_ref, bg_ref, hn_ref, o_ref,
                state_ref, gc_ref, low_ref, qk_ref, u_ref, w_ref, qg_ref, kdt_ref, gam_ref):
    s = q_ref.shape[1]
    n_chunks = s // CHUNK
    ri = lax.broadcasted_iota(jnp.int32, (CHUNK, CHUNK), 0)
    ci = lax.broadcasted_iota(jnp.int32, (CHUNK, CHUNK), 1)
    tril = ci <= ri
    strict = ci < ri

    def mm(a, b):
        return jnp.dot(a.astype(BF16), b.astype(BF16), preferred_element_type=F32)

    def chunk_rows(c):
        return pl.ds(pl.multiple_of(c * CHUNK, CHUNK), CHUNK)

    def head_lanes(h):
        return slice(h * LANES, (h + 1) * LANES)

    def phase_a(c, carry):
        rows = chunk_rows(c)
        bg = bg_ref[0, rows, :]
        gcum = _prefix_sum_rows(bg, CHUNK)
        gc_ref[rows, :] = gcum
        gcum_t = gcum.T
        prods = []
        for h in range(GDN_HEADS):
            q = q_ref[0, rows, head_lanes(h)]
            k = k_ref[0, rows, head_lanes(h)]
            k_beta = (k.astype(F32) * bg[:, h:h + 1]).astype(BF16)
            prods.append(lax.dot_general(
                jnp.concatenate([q, k_beta], axis=0), k,
                (((1,), (1,)), ((), ())), preferred_element_type=F32))
        for h in range(GDN_HEADS):
            n = c * GDN_HEADS + h
            gcol = gcum[:, GDN_HEADS + h:GDN_HEADS + h + 1]
            grow = gcum_t[GDN_HEADS + h:GDN_HEADS + h + 1, :]
            decay = jnp.exp(jnp.where(tril, gcol - grow, -jnp.inf))
            qk_ref[n] = jnp.where(tril, prods[h][:CHUNK] * decay, 0.0).astype(BF16)
            low = jnp.where(strict, prods[h][CHUNK:] * decay, 0.0)
            low_ref[n] = jnp.concatenate([low, jnp.zeros_like(low)], axis=1)
        return carry

    lax.fori_loop(0, n_chunks, phase_a, 0, unroll=GDN_UNROLL)

    rowl_ref = u_ref.at[0:CHUNK]
    tinv_ref = u_ref.at[CHUNK:2 * CHUNK]
    by_row = pltpu.einshape("nij->inj", low_ref[...])
    for i in range(CHUNK):
        rowl_ref[i] = by_row[i].T[0:CHUNK, :]
    sub = lax.broadcasted_iota(jnp.int32, (8, LANES), 0)
    for i in range(CHUNK):
        groups = i // 8 + 1
        acc = [[None, None] for _ in range(groups)]
        for j in range(i):
            coef = rowl_ref[i, j:j + 1, :]
            for g in range(j // 8 + 1):
                term = coef * tinv_ref[j, 8 * g:8 * g + 8, :]
                slot = acc[g]
                slot[j % 2] = term if slot[j % 2] is None else slot[j % 2] + term
        for g in range(CHUNK // 8):
            if g < groups:
                val = jnp.where(sub == i % 8, 1.0, 0.0) if g == groups - 1 else jnp.zeros((8, LANES), F32)
                for part in acc[g]:
                    if part is not None:
                        val = val - part
            else:
                val = jnp.zeros((8, LANES), F32)
            tinv_ref[i, 8 * g:8 * g + 8, :] = val
    pad = jnp.zeros((LANES - CHUNK, LANES), F32)
    t_rows = jnp.stack([jnp.concatenate([tinv_ref[i], pad], axis=0).T for i in range(CHUNK)], axis=0)
    low_ref[...] = pltpu.einshape("inj->nij", t_rows)

    def phase_c(c, carry):
        rows = chunk_rows(c)
        bg = bg_ref[0, rows, :]
        gcum = gc_ref[rows, :]
        heads = range(GDN_HEADS)
        gcols = [gcum[:, GDN_HEADS + h:GDN_HEADS + h + 1] for h in heads]
        e_gs = [jnp.exp(g) for g in gcols]
        kfs = [k_ref[0, rows, head_lanes(h)].astype(F32) for h in heads]
        xs = []
        for h in heads:
            beta = bg[:, h:h + 1]
            v_beta = v_ref[0, rows, head_lanes(h)].astype(F32) * beta
            rhs = jnp.concatenate([v_beta, kfs[h] * beta * e_gs[h]], axis=1)
            xs.append(mm(low_ref[c * GDN_HEADS + h][:, 0:CHUNK], rhs))
        for h in heads:
            n = c * GDN_HEADS + h
            glast = gcols[h][CHUNK - 1:CHUNK, :]
            u_ref[n] = xs[h][:, :LANES]
            w_ref[n] = xs[h][:, LANES:].astype(BF16)
            qg_ref[n] = (q_ref[0, rows, head_lanes(h)].astype(F32) * e_gs[h]).astype(BF16)
            kdt_ref[n] = (kfs[h] * jnp.exp(glast - gcols[h])).T.astype(BF16)
            gam_ref[n] = jnp.broadcast_to(jnp.exp(glast), (1, LANES))
        return carry

    lax.fori_loop(0, n_chunks, phase_c, 0, unroll=GDN_UNROLL)

    state_ref[...] = jnp.zeros_like(state_ref)

    def phase_d(c, carry):
        rows = chunk_rows(c)
        heads = range(GDN_HEADS)
        ns = [c * GDN_HEADS + h for h in heads]
        dot = functools.partial(jnp.dot, preferred_element_type=F32)
        sts = [state_ref[h] for h in heads]
        st_bs = [st.astype(BF16) for st in sts]
        w_st = [dot(w_ref[ns[h]], st_bs[h]) for h in heads]
        q_st = [dot(qg_ref[ns[h]], st_bs[h]) for h in heads]
        v_bs = [(u_ref[ns[h]] - w_st[h]).astype(BF16) for h in heads]
        k_v = [dot(kdt_ref[ns[h]], v_bs[h]) for h in heads]
        qk_v = [dot(qk_ref[ns[h]], v_bs[h]) for h in heads]
        for h in heads:
            state_ref[h] = sts[h] * gam_ref[ns[h]] + k_v[h]
        for h in heads:
            o = _rmsnorm_rows(q_st[h] + qk_v[h], hn_ref[...])
            gate = _silu(z_ref[0, rows, head_lanes(h)].astype(F32))
            o_ref[0, rows, head_lanes(h)] = (o * gate).astype(BF16)
        return carry

    lax.fori_loop(0, n_chunks, phase_d, 0, unroll=GDN_UNROLL)


def _gdn(q, k, v, z, bg, hn):
    b, s, width = q.shape
    n_sys = s // CHUNK * GDN_HEADS
    assert n_sys == LANES, "the lane-batched triangular inverse needs exactly 128 (chunk, head) systems"
    spec = pl.BlockSpec((1, s, width), lambda bi: (bi, 0, 0))
    return pl.pallas_call(
        _gdn_kernel,
        grid=(b,),
        in_specs=[spec, spec, spec, spec,
                  pl.BlockSpec((1, s, LANES), lambda bi: (bi, 0, 0)),
                  pl.BlockSpec((1, LANES), lambda bi: (0, 0))],
        out_specs=spec,
        out_shape=jax.ShapeDtypeStruct((b, s, width), BF16),
        scratch_shapes=[
            pltpu.VMEM((GDN_HEADS, LANES, LANES), F32),
            pltpu.VMEM((s, LANES), F32),
            pltpu.VMEM((n_sys, CHUNK, LANES), F32),
            pltpu.VMEM((n_sys, CHUNK, CHUNK), BF16),
            pltpu.VMEM((n_sys, CHUNK, LANES), F32),
            pltpu.VMEM((n_sys, CHUNK, LANES), BF16),
            pltpu.VMEM((n_sys, CHUNK, LANES), BF16),
            pltpu.VMEM((n_sys, LANES, CHUNK), BF16),
            pltpu.VMEM((n_sys, 1, LANES), F32),
        ],
        compiler_params=pltpu.CompilerParams(
            dimension_semantics=("arbitrary",), vmem_limit_bytes=GDN_VMEM_LIMIT),
        name="gdn",
    )(q, k, v, z, bg, hn)


def _out_proj_kernel(*refs, n_parts):
    x_ref = refs[0]
    parts = refs[1:1 + n_parts]
    w_ref, pw_ref, o_ref = refs[1 + n_parts:]
    y = None
    lo = 0
    for part in parts:
        n = part.shape[1]
        d = jnp.dot(part[...], w_ref[lo:lo + n, :], preferred_element_type=F32)
        y = d if y is None else y + d
        lo += n
    o_ref[...] = x_ref[...] + _rmsnorm_rows(y, pw_ref[...])


def _out_proj(x2, parts, w, pw, name):
    t, dm = x2.shape
    tm = TM_OUT
    row = lambda i: (i, 0)
    fix = lambda i: (0, 0)
    return pl.pallas_call(
        functools.partial(_out_proj_kernel, n_parts=len(parts)),
        grid=(t // tm,),
        in_specs=[pl.BlockSpec((tm, dm), row)]
        + [pl.BlockSpec((tm, p.shape[1]), row) for p in parts]
        + [pl.BlockSpec(w.shape, fix), pl.BlockSpec((1, dm), fix)],
        out_specs=pl.BlockSpec((tm, dm), row),
        out_shape=jax.ShapeDtypeStruct((t, dm), F32),
        compiler_params=pltpu.CompilerParams(
            dimension_semantics=("arbitrary",), vmem_limit_bytes=VMEM_LIMIT),
        name=name,
    )(x2, *parts, w, pw)


def _pad_lanes(vec, offset=0):
    out = jnp.zeros((1, LANES), F32)
    return out.at[0, offset:offset + vec.shape[0]].set(vec.astype(F32))


def kernel(x, positions, pre_norm, post_norm, w_in_ab, a_lambda_q1, a_lambda_k1, a_lambda_q2,
           a_lambda_k2, a_subln, b_conv_w, b_a_log, b_dt_bias, b_head_norm, w_out_ab,
           w_in_c, c_forget_bias, w_out_c):
    b, s, dm = x.shape
    t = b * s
    x2 = x.reshape(t, dm)

    main = 8 * SEC
    w_ab = jnp.concatenate(
        [w_in_ab[:, :main], jnp.pad(w_in_ab[:, main:], ((0, 0), (0, LANES - 2 * GDN_HEADS)))],
        axis=1).astype(BF16)
    inv_freq = ROPE_THETA ** (-(jnp.arange(ROT_HALF, dtype=F32) * 2.0) / ROT_DIMS)
    d = jnp.arange(LANES) % SUB_DIM
    invf = jnp.where(d < ROT_DIMS, inv_freq[d % ROT_HALF], 0.0).reshape(1, LANES)
    (a_q, a_k, a_v, a_z, b_q, b_k, b_v, b_z, b_bg) = _proj_ab(
        x2, positions.reshape(t, 1), invf, pre_norm[0:1], w_ab, b_conv_w.astype(F32),
        _pad_lanes(b_a_log, GDN_HEADS), _pad_lanes(b_dt_bias, GDN_HEADS), seq=s)

    lam_vecs = jnp.concatenate(
        [_pad_lanes(v) for v in (a_lambda_q1, a_lambda_k1, a_lambda_q2, a_lambda_k2)], axis=0)
    r3 = lambda a: a.reshape(b, s, a.shape[-1])
    o_a = _attention(
        _diff_attn_kernel, r3(a_q), r3(a_k), r3(a_v), r3(a_z),
        [lam_vecs, a_subln.astype(F32).reshape(1, LANES)],
        [pl.BlockSpec((4, LANES), lambda bi, g: (0, 0)),
         pl.BlockSpec((1, LANES), lambda bi, g: (0, 0))],
        "diff_attn")
    o_b = _gdn(r3(b_q), r3(b_k), r3(b_v), r3(b_z), r3(b_bg),
               b_head_norm.astype(F32).reshape(1, LANES))
    width = 2 * FOX_PAIRS * SUB_DIM
    w_c = jnp.concatenate(
        [w_in_c[:, :4 * width],
         jnp.pad(w_in_c[:, 4 * width:], ((0, 0), (0, LANES - 2 * FOX_PAIRS)))], axis=1).astype(BF16)
    x2, c_q, c_k, c_v, c_z, c_t = _out_ab_proj_c(
        x2, o_a.reshape(t, SEC), o_b.reshape(t, SEC), w_out_ab.astype(BF16), post_norm[0:1],
        pre_norm[1:2], w_c, _pad_lanes(c_forget_bias), seq=s)
    o_c = _attention(
        _fox_attn_kernel, r3(c_q), r3(c_k), r3(c_v), r3(c_z), [c_t],
        [pl.BlockSpec((1, s // TQ, 2 * FOX_PAIRS, TQ), lambda bi, g: (bi, 0, 0, 0))],
        "fox_attn")
    x2 = _out_proj(x2, [o_c.reshape(t, width)], w_out_c.astype(BF16), post_norm[1:2],
                   "out_proj_c")
    return x2.reshape(b, s, dm)
```

```python
import functools
import math

import jax
import jax.numpy as jnp
from jax import lax
from jax.experimental import pallas as pl
from jax.experimental.pallas import tpu as pltpu

F32 = jnp.float32
BF16 = jnp.bfloat16

EPS = 1e-6
ROPE_THETA = 500000.0
ROT_DIMS = 16
ROT_HALF = ROT_DIMS // 2
SUB_DIM = 64
LANES = 128
N_GROUPS = 4
SEC = N_GROUPS * LANES
CONV_K = 4
CHUNK = 64
GDN_HEADS = 4
GDN_UNROLL = 8
FOX_PAIRS = 8
LAMBDA_INIT_0 = 0.8 - 0.6 * math.exp(-0.3 * 0)
VMEM_LIMIT = 48 * 1024 * 1024
GDN_VMEM_LIMIT = 56 * 1024 * 1024

TM_PROJ = 256
TM_OUT = 1024
TQ = 512
LOG2E = math.log2(math.e)
Q_SCALE = SUB_DIM ** -0.5 * LOG2E


def _rmsnorm_rows(x, w):
    return x * lax.rsqrt(jnp.mean(x * x, axis=-1, keepdims=True) + EPS) * w


def _silu(x):
    return x * jax.nn.sigmoid(x)


def _softplus(x):
    return jnp.maximum(x, 0.0) + jnp.log(1.0 + jnp.exp(-jnp.abs(x)))


def _prefix_sum_rows(x, period):
    rows = x.shape[0]
    pos = lax.broadcasted_iota(jnp.int32, x.shape, 0) % period
    d = 1
    while d < min(period, rows):
        x = x + jnp.where(pos >= d, pltpu.roll(x, d, 0), 0.0)
        d *= 2
    return x


def _proj_ab_kernel(x_ref, pos_ref, invf_ref, nw_ref, w_ref, cw_ref, alog_ref, dtb_ref,
                    q_ref, k_ref, v_ref, az_ref, bq_ref, bk_ref, bv_ref, bz_ref, bg_ref,
                    cbuf_ref, *, tiles_per_seq):
    tm = x_ref.shape[0]
    i = pl.program_id(0)
    h = _rmsnorm_rows(x_ref[...], nw_ref[...]).astype(BF16)

    def proj(lo, width):
        return jnp.dot(h, w_ref[:, lo:lo + width], preferred_element_type=F32)

    ang_t = invf_ref[...] * pos_ref[0].astype(F32)
    cos = jnp.cos(ang_t).T
    sin = jnp.sin(ang_t).T
    d = lax.broadcasted_iota(jnp.int32, (1, LANES), 1) % SUB_DIM
    sin_hi = jnp.where((d >= ROT_HALF) & (d < ROT_DIMS), sin, 0.0)
    sin_lo = jnp.where(d < ROT_HALF, -sin, 0.0)

    def rope_store(acc, out_ref, scale):
        for g in range(N_GROUPS):
            a = acc[:, g * LANES:(g + 1) * LANES]
            r = (a * cos + pltpu.roll(a, ROT_HALF, 1) * sin_hi
                 + pltpu.roll(a, LANES - ROT_HALF, 1) * sin_lo)
            out_ref[:, g * LANES:(g + 1) * LANES] = (r * scale).astype(BF16)

    rope_store(proj(0, SEC), q_ref, Q_SCALE)
    rope_store(proj(SEC, SEC), k_ref, 1.0)
    v_ref[...] = proj(2 * SEC, SEC).astype(BF16)
    az_ref[...] = proj(3 * SEC, SEC).astype(BF16)

    @pl.when(i % tiles_per_seq == 0)
    def _():
        cbuf_ref[...] = jnp.zeros_like(cbuf_ref)

    row8 = lax.broadcasted_iota(jnp.int32, (8, SEC), 0)

    def conv_silu(sec):
        lo = sec * SEC
        acc = proj(4 * SEC + lo, SEC)
        prev = cbuf_ref[:, lo:lo + SEC]
        y = cw_ref[CONV_K - 1:CONV_K, lo:lo + SEC] * acc
        for d in range(1, CONV_K):
            sh = pltpu.roll(acc, d, 0)
            top = jnp.where(row8 < d, pltpu.roll(prev, d, 0), sh[0:8])
            sh = jnp.concatenate([top, sh[8:]], axis=0)
            y = y + cw_ref[CONV_K - 1 - d:CONV_K - d, lo:lo + SEC] * sh
        cbuf_ref[:, lo:lo + SEC] = acc[tm - 8:tm]
        return _silu(y)

    def l2norm_store(y, out_ref, scale):
        for g in range(N_GROUPS):
            a = y[:, g * LANES:(g + 1) * LANES]
            a = a * lax.rsqrt(jnp.sum(a * a, axis=-1, keepdims=True) + EPS)
            out_ref[:, g * LANES:(g + 1) * LANES] = (a * scale).astype(BF16)

    l2norm_store(conv_silu(0), bq_ref, LANES ** -0.5)
    l2norm_store(conv_silu(1), bk_ref, 1.0)
    bv_ref[...] = conv_silu(2).astype(BF16)

    bz_ref[...] = proj(7 * SEC, SEC).astype(BF16)

    sm = proj(8 * SEC, LANES)
    beta = jax.nn.sigmoid(sm)
    g = -jnp.exp(alog_ref[...]) * _softplus(sm + dtb_ref[...])
    lane = lax.broadcasted_iota(jnp.int32, (1, LANES), 1)
    bg_ref[...] = jnp.where(lane < GDN_HEADS, beta, g)


def _proj_ab(x2, pos2, invf, nw, w, cw, alog, dtb, *, seq):
    t, dm = x2.shape
    tm = TM_PROJ
    n_in = w.shape[1]
    row = lambda i: (i, 0)
    fix = lambda i: (0, 0)
    wide = lambda dt: jax.ShapeDtypeStruct((t, SEC), dt)
    out_shape = [wide(BF16)] * 8 + [jax.ShapeDtypeStruct((t, LANES), F32)]
    out_specs = [pl.BlockSpec((tm, SEC), row)] * 8 + [pl.BlockSpec((tm, LANES), row)]
    return pl.pallas_call(
        functools.partial(_proj_ab_kernel, tiles_per_seq=seq // tm),
        grid=(t // tm,),
        in_specs=[
            pl.BlockSpec((tm, dm), row),
            pl.BlockSpec((1, 1, tm), lambda i: (i, 0, 0)),
            pl.BlockSpec((LANES, 1), fix),
            pl.BlockSpec((1, dm), fix),
            pl.BlockSpec((dm, n_in), fix),
            pl.BlockSpec((CONV_K, 3 * SEC), fix),
            pl.BlockSpec((1, LANES), fix),
            pl.BlockSpec((1, LANES), fix),
        ],
        out_specs=out_specs,
        out_shape=out_shape,
        scratch_shapes=[pltpu.VMEM((8, 3 * SEC), F32)],
        compiler_params=pltpu.CompilerParams(
            dimension_semantics=("arbitrary",), vmem_limit_bytes=VMEM_LIMIT),
        name="proj_ab",
    )(x2, pos2, invf, nw, w, cw, alog, dtb)


def _out_ab_proj_c_kernel(x_ref, oa_ref, ob_ref, wo_ref, pw_ref, nw_ref, w_ref, fb_ref,
                          x1_ref, q_ref, k_ref, v_ref, z_ref, ct_ref, h_even, h_odd, carry_ref,
                          *, tiles_per_seq):
    tm = x_ref.shape[0]
    s = pl.program_id(0)
    width = q_ref.shape[1]
    n_a = oa_ref.shape[1]

    @pl.when(s == 0)
    def _():
        h_odd[...] = jnp.zeros_like(h_odd)
        carry_ref[...] = jnp.zeros_like(carry_ref)

    def step(h_next_ref, h_ref):
        y = (jnp.dot(oa_ref[...], wo_ref[0:n_a, :], preferred_element_type=F32)
             + jnp.dot(ob_ref[...], wo_ref[n_a:, :], preferred_element_type=F32))
        x1 = x_ref[...] + _rmsnorm_rows(y, pw_ref[...])
        x1_ref[...] = x1
        h_next_ref[...] = _rmsnorm_rows(x1, nw_ref[...]).astype(BF16)
        h = h_ref[...]

        def proj(lo, n):
            return jnp.dot(h, w_ref[:, lo:lo + n], preferred_element_type=F32)

        for sec in range(width // SEC):
            lo = sec * SEC
            q_ref[:, lo:lo + SEC] = (proj(lo, SEC) * Q_SCALE).astype(BF16)
            k_ref[:, lo:lo + SEC] = proj(width + lo, SEC).astype(BF16)
            v_ref[:, lo:lo + SEC] = proj(2 * width + lo, SEC).astype(BF16)
            z_ref[:, lo:lo + SEC] = proj(3 * width + lo, SEC).astype(BF16)

        @pl.when((s + tiles_per_seq - 1) % tiles_per_seq == 0)
        def _():
            carry_ref[...] = jnp.zeros_like(carry_ref)

        f = proj(4 * width, LANES) + fb_ref[...]
        log_f = -_softplus(-f)
        c = _prefix_sum_rows(log_f, tm) + carry_ref[0:1, :]
        carry_ref[...] = jnp.broadcast_to(c[tm - 1:tm, :], carry_ref.shape)
        ct_ref[0, 0] = c.T[0:2 * FOX_PAIRS, :]

    @pl.when(s % 2 == 0)
    def _():
        step(h_even, h_odd)

    @pl.when(s % 2 == 1)
    def _():
        step(h_odd, h_even)


def _out_ab_proj_c(x2, o_a, o_b, w_out, pw, nw, w, fb, *, seq):
    t, dm = x2.shape
    tm = TM_PROJ
    width = 2 * FOX_PAIRS * SUB_DIM
    n_in = w.shape[1]
    n_tiles = t // tm
    tiles = seq // tm
    per_key_tile = TQ // tm
    cur = lambda i: (jnp.minimum(i, n_tiles - 1), 0)
    prev = lambda i: (jnp.maximum(i - 1, 0), 0)
    fix = lambda i: (0, 0)

    def gate_rows(i):
        j = jnp.maximum(i - 1, 0)
        return (j // tiles, (j % tiles) // per_key_tile, 0, j % per_key_tile)

    wide = jax.ShapeDtypeStruct((t, width), BF16)
    return pl.pallas_call(
        functools.partial(_out_ab_proj_c_kernel, tiles_per_seq=tiles),
        grid=(n_tiles + 1,),
        in_specs=[
            pl.BlockSpec((tm, dm), cur),
            pl.BlockSpec((tm, o_a.shape[1]), cur),
            pl.BlockSpec((tm, o_b.shape[1]), cur),
            pl.BlockSpec(w_out.shape, fix),
            pl.BlockSpec((1, dm), fix),
            pl.BlockSpec((1, dm), fix),
            pl.BlockSpec((dm, n_in), fix),
            pl.BlockSpec((1, LANES), fix),
        ],
        out_specs=[pl.BlockSpec((tm, dm), cur)] + [pl.BlockSpec((tm, width), prev)] * 4
        + [pl.BlockSpec((1, 1, 2 * FOX_PAIRS, tm), gate_rows)],
        out_shape=[jax.ShapeDtypeStruct((t, dm), F32)] + [wide] * 4
        + [jax.ShapeDtypeStruct((t // seq, seq // TQ, 2 * FOX_PAIRS, TQ), F32)],
        scratch_shapes=[pltpu.VMEM((tm, dm), BF16), pltpu.VMEM((tm, dm), BF16),
                        pltpu.VMEM((8, LANES), F32)],
        compiler_params=pltpu.CompilerParams(
            dimension_semantics=("arbitrary",), vmem_limit_bytes=VMEM_LIMIT),
        name="out_ab_proj_c",
    )(x2, o_a, o_b, w_out, pw, nw, w, fb)


def _flash_two_softmax(q_ref, k_ref, v_ref, bias_fn, finish):
    t = TQ
    lo_lanes = lax.broadcasted_iota(jnp.int32, (1, LANES), 1) < SUB_DIM
    causal = (lax.broadcasted_iota(jnp.int32, (t, t), 1)
              <= lax.broadcasted_iota(jnp.int32, (t, t), 0))
    for i in range(q_ref.shape[1] // t):
        rows = slice(i * t, (i + 1) * t)
        q = q_ref[0, rows, :]
        zero = jnp.zeros_like(q)
        q2 = jnp.concatenate([jnp.where(lo_lanes, q, zero), jnp.where(lo_lanes, zero, q)], axis=0)
        m = jnp.full((2 * t, 1), -jnp.inf, F32)
        acc = jnp.zeros((2 * t, 2 * LANES), F32)
        for j in range(i + 1):
            keys = slice(j * t, (j + 1) * t)
            s = lax.dot_general(q2, k_ref[0, keys, :], (((1,), (1,)), ((), ())),
                                preferred_element_type=F32)
            halves = [s[:t], s[t:]]
            if bias_fn is not None:
                halves = [halves[c] + bias_fn(j, c) for c in range(2)]
            if j == i:
                halves = [jnp.where(causal, h, -jnp.inf) for h in halves]
            s = jnp.concatenate(halves, axis=0)
            m_new = jnp.maximum(m, jnp.max(s, axis=1, keepdims=True))
            alpha = jnp.exp2(m - m_new)
            p = jnp.exp2(s - m_new).astype(BF16)
            v_aug = jnp.concatenate([v_ref[0, keys, :], jnp.ones((t, LANES), BF16)], axis=1)
            acc = alpha * acc + jnp.dot(p, v_aug, preferred_element_type=F32)
            m = m_new
        finish(rows, lo_lanes, acc[:, LANES:LANES + 1], acc[:, :LANES])


def _diff_attn_kernel(q_ref, k_ref, v_ref, z_ref, lam_ref, subln_ref, o_ref):
    t = TQ
    lv = lam_ref[...]
    lam = (jnp.exp(jnp.sum(lv[0:1] * lv[1:2], axis=-1, keepdims=True))
           - jnp.exp(jnp.sum(lv[2:3] * lv[3:4], axis=-1, keepdims=True)) + LAMBDA_INIT_0)

    def finish(rows, lo_lanes, l, acc):
        o = acc[:t] / l[:t] - lam * (acc[t:] / l[t:])
        o = _rmsnorm_rows(o, subln_ref[...]) * (1.0 - LAMBDA_INIT_0)
        o_ref[0, rows, :] = (o * _silu(z_ref[0, rows, :].astype(F32))).astype(BF16)

    _flash_two_softmax(q_ref, k_ref, v_ref, None, finish)


def _fox_attn_kernel(q_ref, k_ref, v_ref, z_ref, ct_ref, o_ref):
    t = TQ
    hp = pl.program_id(1)

    def bias(j, c):
        return ct_ref[0, j, pl.ds(2 * hp + c, 1), :] * (-LOG2E)

    def finish(rows, lo_lanes, l, acc):
        o = jnp.where(lo_lanes, acc[:t] / l[:t], acc[t:] / l[t:])
        o_ref[0, rows, :] = (o * _silu(z_ref[0, rows, :].astype(F32))).astype(BF16)

    _flash_two_softmax(q_ref, k_ref, v_ref, bias, finish)


def _attention(kernel_fn, q, k, v, z, extras, extra_specs, name):
    b, s, width = q.shape
    spec = pl.BlockSpec((1, s, LANES), lambda bi, g: (bi, 0, g))
    return pl.pallas_call(
        kernel_fn,
        grid=(b, width // LANES),
        in_specs=[spec, spec, spec, spec] + extra_specs,
        out_specs=spec,
        out_shape=jax.ShapeDtypeStruct((b, s, width), BF16),
        compiler_params=pltpu.CompilerParams(
            dimension_semantics=("arbitrary", "arbitrary"), vmem_limit_bytes=VMEM_LIMIT),
        name=name,
    )(q, k, v, z, *extras)


def _gdn_kernel(q_ref, k_ref, v_ref, z_ref, bg_ref, hn_ref, o_ref,
                state_ref, gc_ref, low_ref, qk_ref, u_ref, w_ref, qg_ref, kdt_ref, gam_ref):
    s = q_ref.shape[1]
    n_chunks = s // CHUNK
    ri = lax.broadcasted_iota(jnp.int32, (CHUNK, CHUNK), 0)
    ci = lax.broadcasted_iota(jnp.int32, (CHUNK, CHUNK), 1)
    tril = ci <= ri
    strict = ci < ri

    def mm(a, b):
        return jnp.dot(a.astype(BF16), b.astype(BF16), preferred_element_type=F32)

    def chunk_rows(c):
        return pl.ds(pl.multiple_of(c * CHUNK, CHUNK), CHUNK)

    def head_lanes(h):
        return slice(h * LANES, (h + 1) * LANES)

    def phase_a(c, carry):
        rows = chunk_rows(c)
        bg = bg_ref[0, rows, :]
        gcum = _prefix_sum_rows(bg, CHUNK)
        gc_ref[rows, :] = gcum
        gcum_t = gcum.T
        prods = []
        for h in range(GDN_HEADS):
            q = q_ref[0, rows, head_lanes(h)]
            k = k_ref[0, rows, head_lanes(h)]
            k_beta = (k.astype(F32) * bg[:, h:h + 1]).astype(BF16)
            prods.append(lax.dot_general(
                jnp.concatenate([q, k_beta], axis=0), k,
                (((1,), (1,)), ((), ())), preferred_element_type=F32))
        for h in range(GDN_HEADS):
            n = c * GDN_HEADS + h
            gcol = gcum[:, GDN_HEADS + h:GDN_HEADS + h + 1]
            grow = gcum_t[GDN_HEADS + h:GDN_HEADS + h + 1, :]
            decay = jnp.exp(jnp.where(tril, gcol - grow, -jnp.inf))
            qk_ref[n] = jnp.where(tril, prods[h][:CHUNK] * decay, 0.0).astype(BF16)
            low = jnp.where(strict, prods[h][CHUNK:] * decay, 0.0)
            low_ref[n] = jnp.concatenate([low, jnp.zeros_like(low)], axis=1)
        return carry

    lax.fori_loop(0, n_chunks, phase_a, 0, unroll=GDN_UNROLL)

    rowl_ref = u_ref.at[0:CHUNK]
    tinv_ref = u_ref.at[CHUNK:2 * CHUNK]
    by_row = pltpu.einshape("nij->inj", low_ref[...])
    for i in range(CHUNK):
        rowl_ref[i] = by_row[i].T[0:CHUNK, :]
    sub = lax.broadcasted_iota(jnp.int32, (8, LANES), 0)
    for i in range(CHUNK):
        groups = i // 8 + 1
        acc = [[None, None] for _ in range(groups)]
        for j in range(i):
            coef = rowl_ref[i, j:j + 1, :]
            for g in range(j // 8 + 1):
                term = coef * tinv_ref[j, 8 * g:8 * g + 8, :]
                slot = acc[g]
                slot[j % 2] = term if slot[j % 2] is None else slot[j % 2] + term
        for g in range(CHUNK // 8):
            if g < groups:
                val = jnp.where(sub == i % 8, 1.0, 0.0) if g == groups - 1 else jnp.zeros((8, LANES), F32)
                for part in acc[g]:
                    if part is not None:
                        val = val - part
            else:
                val = jnp.zeros((8, LANES), F32)
            tinv_ref[i, 8 * g:8 * g + 8, :] = val
    pad = jnp.zeros((LANES - CHUNK, LANES), F32)
    t_rows = jnp.stack([jnp.concatenate([tinv_ref[i], pad], axis=0).T for i in range(CHUNK)], axis=0)
    low_ref[...] = pltpu.einshape("inj->nij", t_rows)

    def phase_c(c, carry):
        rows = chunk_rows(c)
        bg = bg_ref[0, rows, :]
        gcum = gc_ref[rows, :]
        heads = range(GDN_HEADS)
        gcols = [gcum[:, GDN_HEADS + h:GDN_HEADS + h + 1] for h in heads]
        e_gs = [jnp.exp(g) for g in gcols]
        kfs = [k_ref[0, rows, head_lanes(h)].astype(F32) for h in heads]
        xs = []
        for h in heads:
            beta = bg[:, h:h + 1]
            v_beta = v_ref[0, rows, head_lanes(h)].astype(F32) * beta
            rhs = jnp.concatenate([v_beta, kfs[h] * beta * e_gs[h]], axis=1)
            xs.append(mm(low_ref[c * GDN_HEADS + h][:, 0:CHUNK], rhs))
        for h in heads:
            n = c * GDN_HEADS + h
            glast = gcols[h][CHUNK - 1:CHUNK, :]
            u_ref[n] = xs[h][:, :LANES]
            w_ref[n] = xs[h][:, LANES:].astype(BF16)
            qg_ref[n] = (q_ref[0, rows, head_lanes(h)].astype(F32) * e_gs[h]).astype(BF16)
            kdt_ref[n] = (kfs[h] * jnp.exp(glast - gcols[h])).T.astype(BF16)
            gam_ref[n] = jnp.broadcast_to(jnp.exp(glast), (1, LANES))
        return carry

    lax.fori_loop(0, n_chunks, phase_c, 0, unroll=GDN_UNROLL)

    state_ref[...] = jnp.zeros_like(state_ref)

    def phase_d(c, carry):
        rows = chunk_rows(c)
        heads = range(GDN_HEADS)
        ns = [c * GDN_HEADS + h for h in heads]
        dot = functools.partial(jnp.dot, preferred_element_type=F32)
        sts = [state_ref[h] for h in heads]
        st_bs = [st.astype(BF16) for st in sts]
        w_st = [dot(w_ref[ns[h]], st_bs[h]) for h in heads]
        q_st = [dot(qg_ref[ns[h]], st_bs[h]) for h in heads]
        v_bs = [(u_ref[ns[h]] - w_st[h]).astype(BF16) for h in heads]
        k_v = [dot(kdt_ref[ns[h]], v_bs[h]) for h in heads]
        qk_v = [dot(qk_ref[ns[h]], v_bs[h]) for h in heads]
        for h in heads:
            state_ref[h] = sts[h] * gam_ref[ns[h]] + k_v[h]
        for h in heads:
            o = _rmsnorm_rows(q_st[h] + qk_v[h], hn_ref[...])
            gate = _silu(z_ref[0, rows, head_lanes(h)].astype(F32))
            o_ref[0, rows, head_lanes(h)] = (o * gate).astype(BF16)
        return carry

    lax.fori_loop(0, n_chunks, phase_d, 0, unroll=GDN_UNROLL)


def _gdn(q, k, v, z, bg, hn):
    b, s, width = q.shape
    n_sys = s // CHUNK * GDN_HEADS
    assert n_sys == LANES, "the lane-batched triangular inverse needs exactly 128 (chunk, head) systems"
    spec = pl.BlockSpec((1, s, width), lambda bi: (bi, 0, 0))
    return pl.pallas_call(
        _gdn_kernel,
        grid=(b,),
        in_specs=[spec, spec, spec, spec,
                  pl.BlockSpec((1, s, LANES), lambda bi: (bi, 0, 0)),
                  pl.BlockSpec((1, LANES), lambda bi: (0, 0))],
        out_specs=spec,
        out_shape=jax.ShapeDtypeStruct((b, s, width), BF16),
        scratch_shapes=[
            pltpu.VMEM((GDN_HEADS, LANES, LANES), F32),
            pltpu.VMEM((s, LANES), F32),
            pltpu.VMEM((n_sys, CHUNK, LANES), F32),
            pltpu.VMEM((n_sys, CHUNK, CHUNK), BF16),
            pltpu.VMEM((n_sys, CHUNK, LANES), F32),
            pltpu.VMEM((n_sys, CHUNK, LANES), BF16),
            pltpu.VMEM((n_sys, CHUNK, LANES), BF16),
            pltpu.VMEM((n_sys, LANES, CHUNK), BF16),
            pltpu.VMEM((n_sys, 1, LANES), F32),
        ],
        compiler_params=pltpu.CompilerParams(
            dimension_semantics=("arbitrary",), vmem_limit_bytes=GDN_VMEM_LIMIT),
        name="gdn",
    )(q, k, v, z, bg, hn)


def _out_proj_kernel(*refs, n_parts):
    x_ref = refs[0]
    parts = refs[1:1 + n_parts]
    w_ref, pw_ref, o_ref = refs[1 + n_parts:]
    y = None
    lo = 0
    for part in parts:
        n = part.shape[1]
        d = jnp.dot(part[...], w_ref[lo:lo + n, :], preferred_element_type=F32)
        y = d if y is None else y + d
        lo += n
    o_ref[...] = x_ref[...] + _rmsnorm_rows(y, pw_ref[...])


def _out_proj(x2, parts, w, pw, name):
    t, dm = x2.shape
    tm = TM_OUT
    row = lambda i: (i, 0)
    fix = lambda i: (0, 0)
    return pl.pallas_call(
        functools.partial(_out_proj_kernel, n_parts=len(parts)),
        grid=(t // tm,),
        in_specs=[pl.BlockSpec((tm, dm), row)]
        + [pl.BlockSpec((tm, p.shape[1]), row) for p in parts]
        + [pl.BlockSpec(w.shape, fix), pl.BlockSpec((1, dm), fix)],
        out_specs=pl.BlockSpec((tm, dm), row),
        out_shape=jax.ShapeDtypeStruct((t, dm), F32),
        compiler_params=pltpu.CompilerParams(
            dimension_semantics=("arbitrary",), vmem_limit_bytes=VMEM_LIMIT),
        name=name,
    )(x2, *parts, w, pw)


def _pad_lanes(vec, offset=0):
    out = jnp.zeros((1, LANES), F32)
    return out.at[0, offset:offset + vec.shape[0]].set(vec.astype(F32))


def kernel(x, positions, pre_norm, post_norm, w_in_ab, a_lambda_q1, a_lambda_k1, a_lambda_q2,
           a_lambda_k2, a_subln, b_conv_w, b_a_log, b_dt_bias, b_head_norm, w_out_ab,
           w_in_c, c_forget_bias, w_out_c):
    b, s, dm = x.shape
    t = b * s
    x2 = x.reshape(t, dm)

    main = 8 * SEC
    w_ab = jnp.concatenate(
        [w_in_ab[:, :main], jnp.pad(w_in_ab[:, main:], ((0, 0), (0, LANES - 2 * GDN_HEADS)))],
        axis=1).astype(BF16)
    inv_freq = ROPE_THETA ** (-(jnp.arange(ROT_HALF, dtype=F32) * 2.0) / ROT_DIMS)
    d = jnp.arange(LANES) % SUB_DIM
    invf = jnp.where(d < ROT_DIMS, inv_freq[d % ROT_HALF], 0.0).reshape(LANES, 1)
    (a_q, a_k, a_v, a_z, b_q, b_k, b_v, b_z, b_bg) = _proj_ab(
        x2, positions.reshape(t // TM_PROJ, 1, TM_PROJ), invf, pre_norm[0:1], w_ab,
        b_conv_w.astype(F32),
        _pad_lanes(b_a_log, GDN_HEADS), _pad_lanes(b_dt_bias, GDN_HEADS), seq=s)

    lam_vecs = jnp.concatenate(
        [_pad_lanes(v) for v in (a_lambda_q1, a_lambda_k1, a_lambda_q2, a_lambda_k2)], axis=0)
    r3 = lambda a: a.reshape(b, s, a.shape[-1])
    o_a = _attention(
        _diff_attn_kernel, r3(a_q), r3(a_k), r3(a_v), r3(a_z),
        [lam_vecs, a_subln.astype(F32).reshape(1, LANES)],
        [pl.BlockSpec((4, LANES), lambda bi, g: (0, 0)),
         pl.BlockSpec((1, LANES), lambda bi, g: (0, 0))],
        "diff_attn")
    o_b = _gdn(r3(b_q), r3(b_k), r3(b_v), r3(b_z), r3(b_bg),
               b_head_norm.astype(F32).reshape(1, LANES))
    width = 2 * FOX_PAIRS * SUB_DIM
    w_c = jnp.concatenate(
        [w_in_c[:, :4 * width],
         jnp.pad(w_in_c[:, 4 * width:], ((0, 0), (0, LANES - 2 * FOX_PAIRS)))], axis=1).astype(BF16)
    x2, c_q, c_k, c_v, c_z, c_t = _out_ab_proj_c(
        x2, o_a.reshape(t, SEC), o_b.reshape(t, SEC), w_out_ab.astype(BF16), post_norm[0:1],
        pre_norm[1:2], w_c, _pad_lanes(c_forget_bias), seq=s)
    o_c = _attention(
        _fox_attn_kernel, r3(c_q), r3(c_k), r3(c_v), r3(c_z), [c_t],
        [pl.BlockSpec((1, s // TQ, 2 * FOX_PAIRS, TQ), lambda bi, g: (bi, 0, 0, 0))],
        "fox_attn")
    x2 = _out_proj(x2, [o_c.reshape(t, width)], w_out_c.astype(BF16), post_norm[1:2],
                   "out_proj_c")
    return x2.reshape(b, s, dm)
```

```python
import functools
import math

import jax
import jax.numpy as jnp
from jax import lax
from jax.experimental import pallas as pl
from jax.experimental.pallas import tpu as pltpu

F32 = jnp.float32
BF16 = jnp.bfloat16

EPS = 1e-6
ROPE_THETA = 500000.0
ROT_DIMS = 16
ROT_HALF = ROT_DIMS // 2
SUB_DIM = 64
LANES = 128
N_GROUPS = 4
SEC = N_GROUPS * LANES
N_SECTIONS = 8
CONV_K = 4
CHUNK = 64
GDN_HEADS = 4
GDN_UNROLL = 8
FOX_PAIRS = 8
LAMBDA_INIT_0 = 0.8 - 0.6 * math.exp(-0.3 * 0)
VMEM_LIMIT = 48 * 1024 * 1024
GDN_VMEM_LIMIT = 56 * 1024 * 1024

TM_PROJ = 256
TM_OUT = 1024
TQ = 512
LOG2E = math.log2(math.e)
Q_SCALE = SUB_DIM ** -0.5 * LOG2E


def _rmsnorm_rows(x, w):
    return x * lax.rsqrt(jnp.mean(x * x, axis=-1, keepdims=True) + EPS) * w


def _silu(x):
    return x * jax.nn.sigmoid(x)


def _softplus(x):
    return jnp.maximum(x, 0.0) + jnp.log(1.0 + jnp.exp(-jnp.abs(x)))


def _prefix_sum_rows(x, period):
    rows = x.shape[0]
    pos = lax.broadcasted_iota(jnp.int32, x.shape, 0) % period
    d = 1
    while d < min(period, rows):
        x = x + jnp.where(pos >= d, pltpu.roll(x, d, 0), 0.0)
        d *= 2
    return x


def _proj_ab_kernel(x_ref, pos_ref, invf_ref, nw_ref, w_ref, cw_ref, alog_ref, dtb_ref,
                    o_ref, bg_ref, cbuf_ref, *, tiles_per_seq):
    q_ref, k_ref, v_ref, az_ref, bq_ref, bk_ref, bv_ref, bz_ref = (
        o_ref.at[:, sec * SEC:(sec + 1) * SEC] for sec in range(N_SECTIONS))
    tm = x_ref.shape[0]
    i = pl.program_id(0)
    h = _rmsnorm_rows(x_ref[...], nw_ref[...]).astype(BF16)

    def proj(lo, width):
        return jnp.dot(h, w_ref[:, lo:lo + width], preferred_element_type=F32)

    ang_t = invf_ref[...] * pos_ref[0].astype(F32)
    cos = jnp.cos(ang_t).T
    sin = jnp.sin(ang_t).T
    d = lax.broadcasted_iota(jnp.int32, (1, LANES), 1) % SUB_DIM
    sin_hi = jnp.where((d >= ROT_HALF) & (d < ROT_DIMS), sin, 0.0)
    sin_lo = jnp.where(d < ROT_HALF, -sin, 0.0)

    def rope_store(acc, out_ref, scale):
        for g in range(N_GROUPS):
            a = acc[:, g * LANES:(g + 1) * LANES]
            r = (a * cos + pltpu.roll(a, ROT_HALF, 1) * sin_hi
                 + pltpu.roll(a, LANES - ROT_HALF, 1) * sin_lo)
            out_ref[:, g * LANES:(g + 1) * LANES] = (r * scale).astype(BF16)

    rope_store(proj(0, SEC), q_ref, Q_SCALE)
    rope_store(proj(SEC, SEC), k_ref, 1.0)
    v_ref[...] = proj(2 * SEC, SEC).astype(BF16)
    az_ref[...] = proj(3 * SEC, SEC).astype(BF16)

    @pl.when(i % tiles_per_seq == 0)
    def _():
        cbuf_ref[...] = jnp.zeros_like(cbuf_ref)

    row8 = lax.broadcasted_iota(jnp.int32, (8, SEC), 0)

    def conv_silu(sec):
        lo = sec * SEC
        acc = proj(4 * SEC + lo, SEC)
        prev = cbuf_ref[:, lo:lo + SEC]
        y = cw_ref[CONV_K - 1:CONV_K, lo:lo + SEC] * acc
        for d in range(1, CONV_K):
            sh = pltpu.roll(acc, d, 0)
            top = jnp.where(row8 < d, pltpu.roll(prev, d, 0), sh[0:8])
            sh = jnp.concatenate([top, sh[8:]], axis=0)
            y = y + cw_ref[CONV_K - 1 - d:CONV_K - d, lo:lo + SEC] * sh
        cbuf_ref[:, lo:lo + SEC] = acc[tm - 8:tm]
        return _silu(y)

    def l2norm_store(y, out_ref, scale):
        for g in range(N_GROUPS):
            a = y[:, g * LANES:(g + 1) * LANES]
            a = a * lax.rsqrt(jnp.sum(a * a, axis=-1, keepdims=True) + EPS)
            out_ref[:, g * LANES:(g + 1) * LANES] = (a * scale).astype(BF16)

    l2norm_store(conv_silu(0), bq_ref, LANES ** -0.5)
    l2norm_store(conv_silu(1), bk_ref, 1.0)
    bv_ref[...] = conv_silu(2).astype(BF16)

    bz_ref[...] = proj(7 * SEC, SEC).astype(BF16)

    sm = proj(8 * SEC, LANES)
    beta = jax.nn.sigmoid(sm)
    g = -jnp.exp(alog_ref[...]) * _softplus(sm + dtb_ref[...])
    lane = lax.broadcasted_iota(jnp.int32, (1, LANES), 1)
    bg_ref[...] = jnp.where(lane < GDN_HEADS, beta, g)


def _proj_ab(x2, pos2, invf, nw, w, cw, alog, dtb, *, seq):
    t, dm = x2.shape
    tm = TM_PROJ
    n_in = w.shape[1]
    row = lambda i: (i, 0)
    fix = lambda i: (0, 0)
    out_shape = [jax.ShapeDtypeStruct((t, N_SECTIONS * SEC), BF16),
                 jax.ShapeDtypeStruct((t, LANES), F32)]
    out_specs = [pl.BlockSpec((tm, N_SECTIONS * SEC), row), pl.BlockSpec((tm, LANES), row)]
    return pl.pallas_call(
        functools.partial(_proj_ab_kernel, tiles_per_seq=seq // tm),
        grid=(t // tm,),
        in_specs=[
            pl.BlockSpec((tm, dm), row),
            pl.BlockSpec((1, 1, tm), lambda i: (i, 0, 0)),
            pl.BlockSpec((LANES, 1), fix),
            pl.BlockSpec((1, dm), fix),
            pl.BlockSpec((dm, n_in), fix),
            pl.BlockSpec((CONV_K, 3 * SEC), fix),
            pl.BlockSpec((1, LANES), fix),
            pl.BlockSpec((1, LANES), fix),
        ],
        out_specs=out_specs,
        out_shape=out_shape,
        scratch_shapes=[pltpu.VMEM((8, 3 * SEC), F32)],
        compiler_params=pltpu.CompilerParams(
            dimension_semantics=("arbitrary",), vmem_limit_bytes=VMEM_LIMIT),
        name="proj_ab",
    )(x2, pos2, invf, nw, w, cw, alog, dtb)


def _out_ab_proj_c_kernel(x_ref, oa_ref, ob_ref, wo_ref, pw_ref, nw_ref, w_ref, fb_ref,
                          x1_ref, q_ref, k_ref, v_ref, z_ref, ct_ref, h_even, h_odd, carry_ref,
                          *, tiles_per_seq):
    tm = x_ref.shape[0]
    s = pl.program_id(0)
    width = q_ref.shape[1]
    n_a = oa_ref.shape[1]

    @pl.when(s == 0)
    def _():
        h_odd[...] = jnp.zeros_like(h_odd)
        carry_ref[...] = jnp.zeros_like(carry_ref)

    def step(h_next_ref, h_ref):
        y = (jnp.dot(oa_ref[...], wo_ref[0:n_a, :], preferred_element_type=F32)
             + jnp.dot(ob_ref[...], wo_ref[n_a:, :], preferred_element_type=F32))
        x1 = x_ref[...] + _rmsnorm_rows(y, pw_ref[...])
        x1_ref[...] = x1
        h_next_ref[...] = _rmsnorm_rows(x1, nw_ref[...]).astype(BF16)
        h = h_ref[...]

        def proj(lo, n):
            return jnp.dot(h, w_ref[:, lo:lo + n], preferred_element_type=F32)

        for sec in range(width // SEC):
            lo = sec * SEC
            q_ref[:, lo:lo + SEC] = (proj(lo, SEC) * Q_SCALE).astype(BF16)
            k_ref[:, lo:lo + SEC] = proj(width + lo, SEC).astype(BF16)
            v_ref[:, lo:lo + SEC] = proj(2 * width + lo, SEC).astype(BF16)
            z_ref[:, lo:lo + SEC] = proj(3 * width + lo, SEC).astype(BF16)

        @pl.when((s + tiles_per_seq - 1) % tiles_per_seq == 0)
        def _():
            carry_ref[...] = jnp.zeros_like(carry_ref)

        f = proj(4 * width, LANES) + fb_ref[...]
        log_f = -_softplus(-f)
        c = _prefix_sum_rows(log_f, tm) + carry_ref[0:1, :]
        carry_ref[...] = jnp.broadcast_to(c[tm - 1:tm, :], carry_ref.shape)
        ct_ref[0, 0] = c.T[0:2 * FOX_PAIRS, :]

    @pl.when(s % 2 == 0)
    def _():
        step(h_even, h_odd)

    @pl.when(s % 2 == 1)
    def _():
        step(h_odd, h_even)


def _out_ab_proj_c(x2, o_a, o_b, w_out, pw, nw, w, fb, *, seq):
    t, dm = x2.shape
    tm = TM_PROJ
    width = 2 * FOX_PAIRS * SUB_DIM
    n_in = w.shape[1]
    n_tiles = t // tm
    tiles = seq // tm
    per_key_tile = TQ // tm
    cur = lambda i: (jnp.minimum(i, n_tiles - 1), 0)
    prev = lambda i: (jnp.maximum(i - 1, 0), 0)
    fix = lambda i: (0, 0)

    def gate_rows(i):
        j = jnp.maximum(i - 1, 0)
        return (j // tiles, (j % tiles) // per_key_tile, 0, j % per_key_tile)

    wide = jax.ShapeDtypeStruct((t, width), BF16)
    return pl.pallas_call(
        functools.partial(_out_ab_proj_c_kernel, tiles_per_seq=tiles),
        grid=(n_tiles + 1,),
        in_specs=[
            pl.BlockSpec((tm, dm), cur),
            pl.BlockSpec((tm, o_a.shape[1]), cur),
            pl.BlockSpec((tm, o_b.shape[1]), cur),
            pl.BlockSpec(w_out.shape, fix),
            pl.BlockSpec((1, dm), fix),
            pl.BlockSpec((1, dm), fix),
            pl.BlockSpec((dm, n_in), fix),
            pl.BlockSpec((1, LANES), fix),
        ],
        out_specs=[pl.BlockSpec((tm, dm), cur)] + [pl.BlockSpec((tm, width), prev)] * 4
        + [pl.BlockSpec((1, 1, 2 * FOX_PAIRS, tm), gate_rows)],
        out_shape=[jax.ShapeDtypeStruct((t, dm), F32)] + [wide] * 4
        + [jax.ShapeDtypeStruct((t // seq, seq // TQ, 2 * FOX_PAIRS, TQ), F32)],
        scratch_shapes=[pltpu.VMEM((tm, dm), BF16), pltpu.VMEM((tm, dm), BF16),
                        pltpu.VMEM((8, LANES), F32)],
        compiler_params=pltpu.CompilerParams(
            dimension_semantics=("arbitrary",), vmem_limit_bytes=VMEM_LIMIT),
        name="out_ab_proj_c",
    )(x2, o_a, o_b, w_out, pw, nw, w, fb)


def _flash_two_softmax(q_ref, k_ref, v_ref, bias_fn, finish):
    t = TQ
    lo_lanes = lax.broadcasted_iota(jnp.int32, (1, LANES), 1) < SUB_DIM
    causal = (lax.broadcasted_iota(jnp.int32, (t, t), 1)
              <= lax.broadcasted_iota(jnp.int32, (t, t), 0))
    for i in range(q_ref.shape[1] // t):
        rows = slice(i * t, (i + 1) * t)
        q = q_ref[0, rows, :]
        zero = jnp.zeros_like(q)
        q2 = jnp.concatenate([jnp.where(lo_lanes, q, zero), jnp.where(lo_lanes, zero, q)], axis=0)
        m = jnp.full((2 * t, 1), -jnp.inf, F32)
        acc = jnp.zeros((2 * t, 2 * LANES), F32)
        for j in range(i + 1):
            keys = slice(j * t, (j + 1) * t)
            s = lax.dot_general(q2, k_ref[0, keys, :], (((1,), (1,)), ((), ())),
                                preferred_element_type=F32)
            halves = [s[:t], s[t:]]
            if bias_fn is not None:
                halves = [halves[c] + bias_fn(j, c) for c in range(2)]
            if j == i:
                halves = [jnp.where(causal, h, -jnp.inf) for h in halves]
            s = jnp.concatenate(halves, axis=0)
            m_new = jnp.maximum(m, jnp.max(s, axis=1, keepdims=True))
            alpha = jnp.exp2(m - m_new)
            p = jnp.exp2(s - m_new).astype(BF16)
            v_aug = jnp.concatenate([v_ref[0, keys, :], jnp.ones((t, LANES), BF16)], axis=1)
            acc = alpha * acc + jnp.dot(p, v_aug, preferred_element_type=F32)
            m = m_new
        finish(rows, lo_lanes, acc[:, LANES:LANES + 1], acc[:, :LANES])


def _diff_attn_kernel(q_ref, k_ref, v_ref, z_ref, lam_ref, subln_ref, o_ref):
    t = TQ
    lv = lam_ref[...]
    lam = (jnp.exp(jnp.sum(lv[0:1] * lv[1:2], axis=-1, keepdims=True))
           - jnp.exp(jnp.sum(lv[2:3] * lv[3:4], axis=-1, keepdims=True)) + LAMBDA_INIT_0)

    def finish(rows, lo_lanes, l, acc):
        o = acc[:t] / l[:t] - lam * (acc[t:] / l[t:])
        o = _rmsnorm_rows(o, subln_ref[...]) * (1.0 - LAMBDA_INIT_0)
        o_ref[0, rows, :] = (o * _silu(z_ref[0, rows, :].astype(F32))).astype(BF16)

    _flash_two_softmax(q_ref, k_ref, v_ref, None, finish)


def _fox_attn_kernel(q_ref, k_ref, v_ref, z_ref, ct_ref, o_ref):
    t = TQ
    hp = pl.program_id(1)

    def bias(j, c):
        return ct_ref[0, j, pl.ds(2 * hp + c, 1), :] * (-LOG2E)

    def finish(rows, lo_lanes, l, acc):
        o = jnp.where(lo_lanes, acc[:t] / l[:t], acc[t:] / l[t:])
        o_ref[0, rows, :] = (o * _silu(z_ref[0, rows, :].astype(F32))).astype(BF16)

    _flash_two_softmax(q_ref, k_ref, v_ref, bias, finish)


def _attention(kernel_fn, q, k, v, z, extras, extra_specs, name, width=None, offsets=(0, 0, 0, 0)):
    b, s, _ = q.shape
    width = q.shape[2] if width is None else width
    spec = pl.BlockSpec((1, s, LANES), lambda bi, g: (bi, 0, g))
    in_spec = lambda off: pl.BlockSpec((1, s, LANES), lambda bi, g: (bi, 0, off + g))
    return pl.pallas_call(
        kernel_fn,
        grid=(b, width // LANES),
        in_specs=[in_spec(off) for off in offsets] + extra_specs,
        out_specs=spec,
        out_shape=jax.ShapeDtypeStruct((b, s, width), BF16),
        compiler_params=pltpu.CompilerParams(
            dimension_semantics=("arbitrary", "arbitrary"), vmem_limit_bytes=VMEM_LIMIT),
        name=name,
    )(q, k, v, z, *extras)


def _gdn_kernel(q_ref, k_ref, v_ref, z_ref, bg_ref, hn_ref, o_ref,
                state_ref, gc_ref, low_ref, qk_ref, u_ref, w_ref, qg_ref, kdt_ref, gam_ref):
    s = q_ref.shape[1]
    n_chunks = s // CHUNK
    ri = lax.broadcasted_iota(jnp.int32, (CHUNK, CHUNK), 0)
    ci = lax.broadcasted_iota(jnp.int32, (CHUNK, CHUNK), 1)
    tril = ci <= ri
    strict = ci < ri

    def mm(a, b):
        return jnp.dot(a.astype(BF16), b.astype(BF16), preferred_element_type=F32)

    def chunk_rows(c):
        return pl.ds(pl.multiple_of(c * CHUNK, CHUNK), CHUNK)

    def head_lanes(h):
        return slice(h * LANES, (h + 1) * LANES)

    def phase_a(c, carry):
        rows = chunk_rows(c)
        bg = bg_ref[0, rows, :]
        gcum = _prefix_sum_rows(bg, CHUNK)
        gc_ref[rows, :] = gcum
        gcum_t = gcum.T
        prods = []
        for h in range(GDN_HEADS):
            q = q_ref[0, rows, head_lanes(h)]
            k = k_ref[0, rows, head_lanes(h)]
            k_beta = (k.astype(F32) * bg[:, h:h + 1]).astype(BF16)
            prods.append(lax.dot_general(
                jnp.concatenate([q, k_beta], axis=0), k,
                (((1,), (1,)), ((), ())), preferred_element_type=F32))
        for h in range(GDN_HEADS):
            n = c * GDN_HEADS + h
            gcol = gcum[:, GDN_HEADS + h:GDN_HEADS + h + 1]
            grow = gcum_t[GDN_HEADS + h:GDN_HEADS + h + 1, :]
            decay = jnp.exp(jnp.where(tril, gcol - grow, -jnp.inf))
            qk_ref[n] = jnp.where(tril, prods[h][:CHUNK] * decay, 0.0).astype(BF16)
            low = jnp.where(strict, prods[h][CHUNK:] * decay, 0.0)
            low_ref[n] = jnp.concatenate([low, jnp.zeros_like(low)], axis=1)
        return carry

    lax.fori_loop(0, n_chunks, phase_a, 0, unroll=GDN_UNROLL)

    rowl_ref = u_ref.at[0:CHUNK]
    tinv_ref = u_ref.at[CHUNK:2 * CHUNK]
    by_row = pltpu.einshape("nij->inj", low_ref[...])
    for i in range(CHUNK):
        rowl_ref[i] = by_row[i].T[0:CHUNK, :]
    sub = lax.broadcasted_iota(jnp.int32, (8, LANES), 0)
    for i in range(CHUNK):
        groups = i // 8 + 1
        acc = [[None, None] for _ in range(groups)]
        for j in range(i):
            coef = rowl_ref[i, j:j + 1, :]
            for g in range(j // 8 + 1):
                term = coef * tinv_ref[j, 8 * g:8 * g + 8, :]
                slot = acc[g]
                slot[j % 2] = term if slot[j % 2] is None else slot[j % 2] + term
        for g in range(CHUNK // 8):
            if g < groups:
                val = jnp.where(sub == i % 8, 1.0, 0.0) if g == groups - 1 else jnp.zeros((8, LANES), F32)
                for part in acc[g]:
                    if part is not None:
                        val = val - part
            else:
                val = jnp.zeros((8, LANES), F32)
            tinv_ref[i, 8 * g:8 * g + 8, :] = val
    pad = jnp.zeros((LANES - CHUNK, LANES), F32)
    t_rows = jnp.stack([jnp.concatenate([tinv_ref[i], pad], axis=0).T for i in range(CHUNK)], axis=0)
    low_ref[...] = pltpu.einshape("inj->nij", t_rows)

    def phase_c(c, carry):
        rows = chunk_rows(c)
        bg = bg_ref[0, rows, :]
        gcum = gc_ref[rows, :]
        heads = range(GDN_HEADS)
        gcols = [gcum[:, GDN_HEADS + h:GDN_HEADS + h + 1] for h in heads]
        e_gs = [jnp.exp(g) for g in gcols]
        kfs = [k_ref[0, rows, head_lanes(h)].astype(F32) for h in heads]
        xs = []
        for h in heads:
            beta = bg[:, h:h + 1]
            v_beta = v_ref[0, rows, head_lanes(h)].astype(F32) * beta
            rhs = jnp.concatenate([v_beta, kfs[h] * beta * e_gs[h]], axis=1)
            xs.append(mm(low_ref[c * GDN_HEADS + h][:, 0:CHUNK], rhs))
        for h in heads:
            n = c * GDN_HEADS + h
            glast = gcols[h][CHUNK - 1:CHUNK, :]
            u_ref[n] = xs[h][:, :LANES]
            w_ref[n] = xs[h][:, LANES:].astype(BF16)
            qg_ref[n] = (q_ref[0, rows, head_lanes(h)].astype(F32) * e_gs[h]).astype(BF16)
            kdt_ref[n] = (kfs[h] * jnp.exp(glast - gcols[h])).T.astype(BF16)
            gam_ref[n] = jnp.broadcast_to(jnp.exp(glast), (1, LANES))
        return carry

    lax.fori_loop(0, n_chunks, phase_c, 0, unroll=GDN_UNROLL)

    state_ref[...] = jnp.zeros_like(state_ref)

    def phase_d(c, carry):
        rows = chunk_rows(c)
        heads = range(GDN_HEADS)
        ns = [c * GDN_HEADS + h for h in heads]
        dot = functools.partial(jnp.dot, preferred_element_type=F32)
        sts = [state_ref[h] for h in heads]
        st_bs = [st.astype(BF16) for st in sts]
        w_st = [dot(w_ref[ns[h]], st_bs[h]) for h in heads]
        q_st = [dot(qg_ref[ns[h]], st_bs[h]) for h in heads]
        v_bs = [(u_ref[ns[h]] - w_st[h]).astype(BF16) for h in heads]
        k_v = [dot(kdt_ref[ns[h]], v_bs[h]) for h in heads]
        qk_v = [dot(qk_ref[ns[h]], v_bs[h]) for h in heads]
        for h in heads:
            state_ref[h] = sts[h] * gam_ref[ns[h]] + k_v[h]
        for h in heads:
            o = _rmsnorm_rows(q_st[h] + qk_v[h], hn_ref[...])
            gate = _silu(z_ref[0, rows, head_lanes(h)].astype(F32))
            o_ref[0, rows, head_lanes(h)] = (o * gate).astype(BF16)
        return carry

    lax.fori_loop(0, n_chunks, phase_d, 0, unroll=GDN_UNROLL)


def _gdn(q, k, v, z, bg, hn, width, offsets):
    b, s, _ = q.shape
    n_sys = s // CHUNK * GDN_HEADS
    assert n_sys == LANES, "the lane-batched triangular inverse needs exactly 128 (chunk, head) systems"
    spec = pl.BlockSpec((1, s, width), lambda bi: (bi, 0, 0))
    in_spec = lambda off: pl.BlockSpec((1, s, width), lambda bi: (bi, 0, off))
    return pl.pallas_call(
        _gdn_kernel,
        grid=(b,),
        in_specs=[in_spec(off) for off in offsets] + [
                  pl.BlockSpec((1, s, LANES), lambda bi: (bi, 0, 0)),
                  pl.BlockSpec((1, LANES), lambda bi: (0, 0))],
        out_specs=spec,
        out_shape=jax.ShapeDtypeStruct((b, s, width), BF16),
        scratch_shapes=[
            pltpu.VMEM((GDN_HEADS, LANES, LANES), F32),
            pltpu.VMEM((s, LANES), F32),
            pltpu.VMEM((n_sys, CHUNK, LANES), F32),
            pltpu.VMEM((n_sys, CHUNK, CHUNK), BF16),
            pltpu.VMEM((n_sys, CHUNK, LANES), F32),
            pltpu.VMEM((n_sys, CHUNK, LANES), BF16),
            pltpu.VMEM((n_sys, CHUNK, LANES), BF16),
            pltpu.VMEM((n_sys, LANES, CHUNK), BF16),
            pltpu.VMEM((n_sys, 1, LANES), F32),
        ],
        compiler_params=pltpu.CompilerParams(
            dimension_semantics=("arbitrary",), vmem_limit_bytes=GDN_VMEM_LIMIT),
        name="gdn",
    )(q, k, v, z, bg, hn)


def _out_proj_kernel(*refs, n_parts):
    x_ref = refs[0]
    parts = refs[1:1 + n_parts]
    w_ref, pw_ref, o_ref = refs[1 + n_parts:]
    y = None
    lo = 0
    for part in parts:
        n = part.shape[1]
        d = jnp.dot(part[...], w_ref[lo:lo + n, :], preferred_element_type=F32)
        y = d if y is None else y + d
        lo += n
    o_ref[...] = x_ref[...] + _rmsnorm_rows(y, pw_ref[...])


def _out_proj(x2, parts, w, pw, name):
    t, dm = x2.shape
    tm = TM_OUT
    row = lambda i: (i, 0)
    fix = lambda i: (0, 0)
    return pl.pallas_call(
        functools.partial(_out_proj_kernel, n_parts=len(parts)),
        grid=(t // tm,),
        in_specs=[pl.BlockSpec((tm, dm), row)]
        + [pl.BlockSpec((tm, p.shape[1]), row) for p in parts]
        + [pl.BlockSpec(w.shape, fix), pl.BlockSpec((1, dm), fix)],
        out_specs=pl.BlockSpec((tm, dm), row),
        out_shape=jax.ShapeDtypeStruct((t, dm), F32),
        compiler_params=pltpu.CompilerParams(
            dimension_semantics=("arbitrary",), vmem_limit_bytes=VMEM_LIMIT),
        name=name,
    )(x2, *parts, w, pw)


def _pad_lanes(vec, offset=0):
    out = jnp.zeros((1, LANES), F32)
    return out.at[0, offset:offset + vec.shape[0]].set(vec.astype(F32))


def kernel(x, positions, pre_norm, post_norm, w_in_ab, a_lambda_q1, a_lambda_k1, a_lambda_q2,
           a_lambda_k2, a_subln, b_conv_w, b_a_log, b_dt_bias, b_head_norm, w_out_ab,
           w_in_c, c_forget_bias, w_out_c):
    b, s, dm = x.shape
    t = b * s
    x2 = x.reshape(t, dm)

    main = 8 * SEC
    w_ab = jnp.concatenate(
        [w_in_ab[:, :main], jnp.pad(w_in_ab[:, main:], ((0, 0), (0, LANES - 2 * GDN_HEADS)))],
        axis=1).astype(BF16)
    inv_freq = ROPE_THETA ** (-(jnp.arange(ROT_HALF, dtype=F32) * 2.0) / ROT_DIMS)
    d = jnp.arange(LANES) % SUB_DIM
    invf = jnp.where(d < ROT_DIMS, inv_freq[d % ROT_HALF], 0.0).reshape(LANES, 1)
    ab, b_bg = _proj_ab(
        x2, positions.reshape(t // TM_PROJ, 1, TM_PROJ), invf, pre_norm[0:1], w_ab,
        b_conv_w.astype(F32),
        _pad_lanes(b_a_log, GDN_HEADS), _pad_lanes(b_dt_bias, GDN_HEADS), seq=s)

    lam_vecs = jnp.concatenate(
        [_pad_lanes(v) for v in (a_lambda_q1, a_lambda_k1, a_lambda_q2, a_lambda_k2)], axis=0)
    r3 = lambda a: a.reshape(b, s, a.shape[-1])
    ab = r3(ab)
    o_a = _attention(
        _diff_attn_kernel, ab, ab, ab, ab,
        [lam_vecs, a_subln.astype(F32).reshape(1, LANES)],
        [pl.BlockSpec((4, LANES), lambda bi, g: (0, 0)),
         pl.BlockSpec((1, LANES), lambda bi, g: (0, 0))],
        "diff_attn", width=SEC, offsets=tuple(sec * N_GROUPS for sec in range(4)))
    o_b = _gdn(ab, ab, ab, ab, r3(b_bg), b_head_norm.astype(F32).reshape(1, LANES),
               width=SEC, offsets=(4, 5, 6, 7))
    width = 2 * FOX_PAIRS * SUB_DIM
    w_c = jnp.concatenate(
        [w_in_c[:, :4 * width],
         jnp.pad(w_in_c[:, 4 * width:], ((0, 0), (0, LANES - 2 * FOX_PAIRS)))], axis=1).astype(BF16)
    x2, c_q, c_k, c_v, c_z, c_t = _out_ab_proj_c(
        x2, o_a.reshape(t, SEC), o_b.reshape(t, SEC), w_out_ab.astype(BF16), post_norm[0:1],
        pre_norm[1:2], w_c, _pad_lanes(c_forget_bias), seq=s)
    o_c = _attention(
        _fox_attn_kernel, r3(c_q), r3(c_k), r3(c_v), r3(c_z), [c_t],
        [pl.BlockSpec((1, s // TQ, 2 * FOX_PAIRS, TQ), lambda bi, g: (bi, 0, 0, 0))],
        "fox_attn")
    x2 = _out_proj(x2, [o_c.reshape(t, width)], w_out_c.astype(BF16), post_norm[1:2],
                   "out_proj_c")
    return x2.reshape(b, s, dm)
```
